```python
import math
import jax, jax.numpy as jnp
from jax import lax
import numpy as np

D_MODEL = 2048
BATCH = 32
SEQ = 256
DEPTH = 4
DEC_BATCH = 8
DEC_SEQ = 4096
PAST_LEN = 256

GRID_W = 64
EPS = 1e-6
POOL_W = 512
POOL_WINDOWS = (2, 4, 8, 16)
POOL_GROUPS = len(POOL_WINDOWS)
POOL_GW = POOL_W // POOL_GROUPS
N_HEADS = 8
N_KV = 2
HEAD_DIM = 128
GQA_G = N_HEADS // N_KV
ATTN_W = N_HEADS * HEAD_DIM
KV_W = N_KV * HEAD_DIM
WINDOW = 128
BLOCK = 128
ROPE_BASE = 10000.0
LRU_W = 512
LRU_BLOCKS = 4
LRU_BW = LRU_W // LRU_BLOCKS
CONV_W = 4
CONV_LEFT = 2
LRU_C = 8.0
N_BRANCH = 3
IN_SPLITS = (POOL_W, POOL_W, ATTN_W, KV_W, KV_W, ATTN_W, LRU_W, LRU_W, N_BRANCH * D_MODEL)
IN_W = sum(IN_SPLITS)
IN_OFFSETS = tuple(sum(IN_SPLITS[:i + 1]) for i in range(len(IN_SPLITS) - 1))

kernel_name = 'hybrid_diffusion_prefix_pool_swa_rglru_step'

F32 = jnp.float32


def rms_norm(x, g):
    xf = x.astype(F32)
    y = xf * lax.rsqrt(jnp.mean(xf * xf, axis=-1, keepdims=True) + EPS)
    return (y * g.astype(F32)).astype(x.dtype)


def pool_mixer(u, w_map, scale):
    B, T, _ = u.shape
    uf = u.astype(F32)
    csum = jnp.pad(jnp.cumsum(uf, axis=1), ((0, 0), (1, 0), (0, 0)))
    t = jnp.arange(T)
    groups = []
    for gi, w in enumerate(POOL_WINDOWS):
        sl = slice(gi * POOL_GW, (gi + 1) * POOL_GW)
        lo = jnp.clip(t - w // 2, 0, T)
        hi = jnp.clip(t + w // 2, 0, T)
        cs = csum[..., sl]
        mean = (cs[:, hi] - cs[:, lo]) / (hi - lo).astype(F32)[None, :, None]
        groups.append(mean - uf[..., sl])
    d = jnp.stack(groups, axis=2)
    y = jnp.einsum('btgc,gce->btge', d, w_map.astype(F32)).reshape(B, T, POOL_W)
    return (y * scale.astype(F32)).astype(u.dtype)


def axial_rope_tables(T):
    rows = T // GRID_W
    row = jnp.repeat(jnp.arange(rows, dtype=F32), GRID_W)
    col = jnp.tile(jnp.arange(GRID_W, dtype=F32), rows)
    n_freq = HEAD_DIM // 4
    inv = ROPE_BASE ** (-jnp.arange(n_freq, dtype=F32) / n_freq)
    ang_r = row[:, None] * inv[None]
    ang_c = col[:, None] * inv[None]
    return jnp.cos(ang_r), jnp.sin(ang_r), jnp.cos(ang_c), jnp.sin(ang_c)


def _rotate(x, cos, sin):
    n = x.shape[-1] // 2
    x1, x2 = x[..., :n], x[..., n:]
    return jnp.concatenate([x1 * cos - x2 * sin, x2 * cos + x1 * sin], axis=-1)


def apply_axial_rope(x, tables):
    cos_r, sin_r, cos_c, sin_c = tables
    extra = (1,) * (x.ndim - 3)
    ex = lambda a: a.reshape((1, a.shape[0]) + extra + (a.shape[1],))
    xf = x.astype(F32)
    half = HEAD_DIM // 2
    xr = _rotate(xf[..., :half], ex(cos_r), ex(sin_r))
    xc = _rotate(xf[..., half:], ex(cos_c), ex(sin_c))
    return jnp.concatenate([xr, xc], axis=-1).astype(x.dtype)


def context_attention(q, k, v, sink):
    B, T = q.shape[:2]
    nb = T // BLOCK
    scale = HEAD_DIM ** -0.5
    qb = jnp.moveaxis(q.reshape(B, nb, BLOCK, N_KV, GQA_G, HEAD_DIM), 1, 0)

    def one_block(qi):
        s = jnp.einsum('bqkgd,bskd->bkgqs', qi, k, preferred_element_type=F32) * scale
        sk = jnp.broadcast_to(sink.astype(F32)[None, :, :, None, None], s.shape[:-1] + (1,))
        p = jax.nn.softmax(jnp.concatenate([s, sk], axis=-1), axis=-1)[..., :-1]
        return jnp.einsum('bkgqs,bskd->bqkgd', p.astype(v.dtype), v)

    out = lax.map(one_block, qb)
    return jnp.moveaxis(out, 0, 1).reshape(B, T, ATTN_W)


def latent_attention(q, k, v, k_ctx, v_ctx, sink):
    B, T = q.shape[:2]
    nb = T // BLOCK
    span = BLOCK + 2 * WINDOW
    scale = HEAD_DIM ** -0.5
    pad = ((0, 0), (WINDOW, WINDOW), (0, 0), (0, 0))
    kp = jnp.pad(k, pad)
    vp = jnp.pad(v, pad)
    qb = jnp.moveaxis(q.reshape(B, nb, BLOCK, N_KV, GQA_G, HEAD_DIM), 1, 0)

    def one_block(args):
        qi, bi = args
        start = bi * BLOCK
        kw = lax.dynamic_slice_in_dim(kp, start, span, axis=1)
        vw = lax.dynamic_slice_in_dim(vp, start, span, axis=1)
        qpos = start + jnp.arange(BLOCK)
        kpos = start - WINDOW + jnp.arange(span)
        valid = (jnp.abs(qpos[:, None] - kpos[None, :]) <= WINDOW) & (kpos >= 0)[None, :] & (kpos < T)[None, :]
        s_w = jnp.einsum('bqkgd,bskd->bkgqs', qi, kw, preferred_element_type=F32) * scale
        s_w = jnp.where(valid, s_w, -jnp.inf)
        s_c = jnp.einsum('bqkgd,bskd->bkgqs', qi, k_ctx, preferred_element_type=F32) * scale
        sk = jnp.broadcast_to(sink.astype(F32)[None, :, :, None, None], s_w.shape[:-1] + (1,))
        p = jax.nn.softmax(jnp.concatenate([s_w, s_c, sk], axis=-1), axis=-1)
        p_w = p[..., :span].astype(v.dtype)
        p_c = p[..., span:-1].astype(v.dtype)
        return (jnp.einsum('bkgqs,bskd->bqkgd', p_w, vw)
                + jnp.einsum('bkgqs,bskd->bqkgd', p_c, v_ctx))

    out = lax.map(one_block, (qb, jnp.arange(nb)))
    return jnp.moveaxis(out, 0, 1).reshape(B, T, ATTN_W)


def _linear_scan(a, b, h0, reverse):
    idx = -1 if reverse else 0
    b = b.at[:, idx].add(a[:, idx] * h0)

    def combine(e1, e2):
        a1, b1 = e1
        a2, b2 = e2
        return a1 * a2, a2 * b1 + b2

    _, h = lax.associative_scan(combine, (a, b), reverse=reverse, axis=1)
    return h


def rglru_branch(u, conv_w, conv_b, gate_w, gate_b, lam, h0):
    B, T, _ = u.shape
    uf = u.astype(F32)
    up = jnp.pad(uf, ((0, 0), (CONV_LEFT, CONV_W - 1 - CONV_LEFT), (0, 0)))
    cw = conv_w.astype(F32)
    xc = conv_b.astype(F32) + up[:, 0:T] * cw[0]
    for j in range(1, CONV_W):
        xc = xc + up[:, j:j + T] * cw[j]
    g = jnp.einsum('btnc,dknce->dkbtne', xc.reshape(B, T, LRU_BLOCKS, LRU_BW), gate_w.astype(F32))
    g = g.reshape(2, 2, B, T, LRU_W) + gate_b.astype(F32)[:, :, None, None, :]
    r = jax.nn.sigmoid(g[:, 0])
    i = jax.nn.sigmoid(g[:, 1])
    log_a = -LRU_C * r * jax.nn.softplus(-lam.astype(F32))[:, None, None, :]
    a = jnp.exp(log_a)
    b = jnp.sqrt(jnp.maximum(-jnp.expm1(2.0 * log_a), 0.0)) * (i * xc[None])
    h0f = h0.astype(F32)
    h_f = _linear_scan(a[0], b[0], h0f[:, 0], reverse=False)
    h_b = _linear_scan(a[1], b[1], h0f[:, 1], reverse=True)
    return h_f, h_b


def trunk_layer(x, mod, P, ctx=None):
    B, T, _ = x.shape
    shift, scale, gate = jnp.split(mod, 3, axis=-1)
    h = rms_norm(x, P['g_pre']) * (1 + scale[:, None]) + shift[:, None]
    (u_pool, z_pool, q, k, v, z_attn, u_lru, z_lru, merge_logits) = jnp.split(
        h @ P['w_in'], IN_OFFSETS, axis=-1)
    y_pool = pool_mixer(u_pool, P['w_pool_map'], P['pool_scale']) * jax.nn.silu(z_pool)
    q = q.reshape(B, T, N_KV, GQA_G, HEAD_DIM)
    k = k.reshape(B, T, N_KV, HEAD_DIM)
    v = v.reshape(B, T, N_KV, HEAD_DIM)
    sink = P['attn_sink'].reshape(N_KV, GQA_G)
    if ctx is None:
        o_attn = context_attention(q, k, v, sink)
        h0 = jnp.zeros((B, 2, LRU_W), F32)
    else:
        k_ctx, v_ctx, h0 = ctx
        tables = axial_rope_tables(T)
        q_r = apply_axial_rope(q, tables)
        k_r = apply_axial_rope(k, tables)
        o_attn = latent_attention(q_r, k_r, v, k_ctx.astype(k.dtype), v_ctx.astype(v.dtype), sink)
    y_attn = o_attn.astype(x.dtype) * jax.nn.silu(z_attn)
    h_f, h_b = rglru_branch(u_lru, P['lru_conv_w'], P['lru_conv_b'], P['lru_gate_w'],
                            P['lru_gate_b'], P['lru_lambda'], h0)
    y_lru = (h_f + h_b).astype(x.dtype) * jax.nn.silu(z_lru)
    m = jax.nn.sigmoid(merge_logits).reshape(B, T, N_BRANCH, D_MODEL)
    merged = (m[:, :, 0] * (y_pool @ P['w_pool_o'])
              + m[:, :, 1] * (y_attn @ P['w_attn_o'])
              + m[:, :, 2] * (y_lru @ P['w_lru_o']))
    out = merged @ P['w_out']
    x_new = x + gate[:, None] * rms_norm(out, P['g_post'])
    if ctx is None:
        h_final = jnp.stack([h_f[:, -1], h_b[:, 0]], axis=1)
        return x_new, k, v, h_final
    return x_new


def setup_inputs(seed: int = 0) -> dict:
    key = jax.random.key(seed)
    ks = jax.random.split(key, 24)
    nrm = lambda i, shape, s: jax.random.normal(ks[i], shape, F32) * s
    a0 = jax.random.uniform(ks[19], (DEPTH, 2, LRU_W), F32, 0.9, 0.999)
    return {
        'x_prompt': nrm(0, (BATCH, SEQ, D_MODEL), 1.0),
        'x_sample': nrm(1, (DEC_BATCH, DEC_SEQ, D_MODEL), 1.0),
        'cache_k': nrm(2, (DEC_BATCH, DEPTH, PAST_LEN, N_KV, HEAD_DIM), 1.0),
        'cache_v': nrm(3, (DEC_BATCH, DEPTH, PAST_LEN, N_KV, HEAD_DIM), 1.0),
        'state_lru': nrm(4, (DEC_BATCH, DEPTH, 2, LRU_W), 0.5),
        'c': nrm(5, (DEC_BATCH, D_MODEL), 1.0),
        'c_ctx': nrm(6, (D_MODEL,), 1.0),
        'g_pre': 1.0 + nrm(7, (DEPTH, D_MODEL), 0.02),
        'g_post': 1.0 + nrm(8, (DEPTH, D_MODEL), 0.02),
        'w_ada': nrm(9, (DEPTH, D_MODEL, 3 * D_MODEL), 0.3 * D_MODEL ** -0.5),
        'b_ada': nrm(10, (DEPTH, 3 * D_MODEL), 0.01),
        'w_in': nrm(11, (DEPTH, D_MODEL, IN_W), D_MODEL ** -0.5),
        'w_pool_map': nrm(12, (DEPTH, POOL_GROUPS, POOL_GW, POOL_GW), POOL_GW ** -0.5),
        'pool_scale': 1.0 + nrm(13, (DEPTH, POOL_W), 0.1),
        'attn_sink': nrm(14, (DEPTH, N_HEADS), 0.5),
        'lru_conv_w': nrm(15, (DEPTH, CONV_W, LRU_W), CONV_W ** -0.5),
        'lru_conv_b': nrm(16, (DEPTH, LRU_W), 0.01),
        'lru_gate_w': nrm(17, (DEPTH, 2, 2, LRU_BLOCKS, LRU_BW, LRU_BW), LRU_BW ** -0.5),
        'lru_gate_b': nrm(18, (DEPTH, 2, 2, LRU_W), 0.01),
        'lru_lambda': jnp.log(a0) - jnp.log1p(-a0),
        'w_pool_o': nrm(20, (DEPTH, POOL_W, D_MODEL), POOL_W ** -0.5),
        'w_attn_o': nrm(21, (DEPTH, ATTN_W, D_MODEL), ATTN_W ** -0.5),
        'w_lru_o': nrm(22, (DEPTH, LRU_W, D_MODEL), LRU_W ** -0.5),
        'w_out': nrm(23, (DEPTH, D_MODEL, D_MODEL), D_MODEL ** -0.5),
    }


def reference(x_prompt, x_sample, cache_k, cache_v, state_lru, c, c_ctx, g_pre, g_post,
              w_ada, b_ada, w_in, w_pool_map, pool_scale, attn_sink, lru_conv_w, lru_conv_b,
              lru_gate_w, lru_gate_b, lru_lambda, w_pool_o, w_attn_o, w_lru_o, w_out):
    xp = x_prompt
    xs = x_sample
    new_k, new_v, new_h = [], [], []
    for l in range(DEPTH):
        P = {
            'g_pre': g_pre[l], 'g_post': g_post[l], 'w_in': w_in[l],
            'w_pool_map': w_pool_map[l], 'pool_scale': pool_scale[l], 'attn_sink': attn_sink[l],
            'lru_conv_w': lru_conv_w[l], 'lru_conv_b': lru_conv_b[l], 'lru_gate_w': lru_gate_w[l],
            'lru_gate_b': lru_gate_b[l], 'lru_lambda': lru_lambda[l],
            'w_pool_o': w_pool_o[l], 'w_attn_o': w_attn_o[l], 'w_lru_o': w_lru_o[l], 'w_out': w_out[l],
        }
        mod_ctx = (jax.nn.silu(c_ctx) @ w_ada[l] + b_ada[l])[None, :]
        mod_lat = jax.nn.silu(c) @ w_ada[l] + b_ada[l]
        xp, k_l, v_l, h_l = trunk_layer(xp, mod_ctx, P)
        new_k.append(k_l)
        new_v.append(v_l)
        new_h.append(h_l.astype(x_prompt.dtype))
        xs = trunk_layer(xs, mod_lat, P, ctx=(cache_k[:, l], cache_v[:, l], state_lru[:, l]))
    return (xp, xs, jnp.stack(new_k, axis=1), jnp.stack(new_v, axis=1), jnp.stack(new_h, axis=1))
```

```python
import functools

import jax
import jax.numpy as jnp
from jax import lax
from jax.experimental import pallas as pl
from jax.experimental.pallas import tpu as pltpu

F32 = jnp.float32
BF16 = jnp.bfloat16

EPS = 1e-6
LANES = 128
SUBLANES = 8
POOL_WINDOWS = (2, 4, 8, 16)
N_HEADS = 8
N_KV = 2
HEAD_DIM = 128
GQA_G = N_HEADS // N_KV
WINDOW = 128
BLOCK = 128
GRID_W = 64
ROPE_BASE = 10000.0
LRU_BLOCKS = 4
CONV_W = 4
CONV_LEFT = 2
LRU_C = 8.0
N_BRANCH = 3
MOD_ROWS = 16

VMEM_LIMIT = 56 * 1024 * 1024


def _sigmoid(x):
    return 0.5 * (jnp.tanh(0.5 * x) + 1.0)


def _silu(x):
    return x * _sigmoid(x)


def _params(sem):
    return pltpu.CompilerParams(dimension_semantics=sem, vmem_limit_bytes=VMEM_LIMIT)


def _mod_kernel(c_ref, w_ref, b_ref, o_ref):
    s = _silu(c_ref[...]).astype(BF16)
    o_ref[...] = jnp.dot(s, w_ref[...].astype(BF16), preferred_element_type=F32) + b_ref[...]


def _modulation(cvec, w_ada, b_ada):
    depth, d, n = w_ada.shape
    tn = 768
    return pl.pallas_call(
        _mod_kernel,
        grid=(depth, n // tn),
        in_specs=[
            pl.BlockSpec((MOD_ROWS, d), lambda l, j: (0, 0)),
            pl.BlockSpec((None, d, tn), lambda l, j: (l, 0, j)),
            pl.BlockSpec((None, 1, tn), lambda l, j: (l, 0, j)),
        ],
        out_specs=pl.BlockSpec((None, MOD_ROWS, tn), lambda l, j: (l, 0, j)),
        out_shape=jax.ShapeDtypeStruct((depth, MOD_ROWS, n), F32),
        compiler_params=_params(("arbitrary", "arbitrary")),
        name="adaln_mod",
    )(cvec, w_ada, b_ada.reshape(depth, 1, n))


def _prenorm(x, g, shift, scale):
    y = x * lax.rsqrt(jnp.mean(x * x, axis=-1, keepdims=True) + EPS) * g
    return y * (1.0 + scale) + shift


def _inproj_kernel(x_ref, mod_ref, g_ref, w_ref, o_ref, h_ref):
    @pl.when(pl.program_id(1) == 0)
    def _():
        h_ref[...] = _prenorm(x_ref[...], g_ref[...], mod_ref[0], mod_ref[1]).astype(BF16)

    o_ref[...] = jnp.dot(h_ref[...], w_ref[...], preferred_element_type=F32)


def _in_projection(x2, mod, g_pre, w_in_b, n_mix, seq, mod_row0, tm, tn):
    rows, d = x2.shape
    if mod_row0 is None:
        mod_map = lambda i, j: ((i * tm) // seq, 0, 0, 0)
    else:
        mod_map = lambda i, j: (mod_row0, 0, 0, 0)
    return pl.pallas_call(
        _inproj_kernel,
        grid=(rows // tm, n_mix // tn),
        in_specs=[
            pl.BlockSpec((tm, d), lambda i, j: (i, 0)),
            pl.BlockSpec((None, 3, 1, d), mod_map),
            pl.BlockSpec((1, d), lambda i, j: (0, 0)),
            pl.BlockSpec((d, tn), lambda i, j: (0, j)),
        ],
        out_specs=pl.BlockSpec((tm, tn), lambda i, j: (i, j)),
        out_shape=jax.ShapeDtypeStruct((rows, n_mix), F32),
        scratch_shapes=[pltpu.VMEM((tm, d), BF16)],
        compiler_params=_params(("arbitrary", "arbitrary")),
        name="in_proj",
    )(x2, mod, g_pre, w_in_b)


def _pool_kernel(up_ref, uc_ref, un_ref, z_ref, wm_ref, sc_ref, o_ref, ext_ref, *, seq, tt):
    i = pl.program_id(1)
    nt = pl.num_programs(1)
    halo = SUBLANES
    ext_ref[0:halo, :] = jnp.where(i > 0, up_ref[...], 0.0)
    ext_ref[halo:halo + tt, :] = uc_ref[...]
    ext_ref[halo + tt:2 * halo + tt, :] = jnp.where(i < nt - 1, un_ref[...], 0.0)
    t = i * tt + lax.broadcasted_iota(jnp.int32, (tt, 1), 0)
    for g, w in enumerate(POOL_WINDOWS):
        lanes = slice(g * LANES, (g + 1) * LANES)
        base = halo - w // 2
        acc = ext_ref[base:base + tt, lanes]
        for k in range(1, w):
            acc = acc + ext_ref[base + k:base + k + tt, lanes]
        cnt = (jnp.minimum(t + w // 2, seq) - jnp.maximum(t - w // 2, 0)).astype(F32)
        dlt = acc / cnt - uc_ref[:, lanes]
        y = jnp.dot(dlt.astype(BF16), wm_ref[g], preferred_element_type=F32)
        y = (y * sc_ref[:, lanes]) * _silu(z_ref[:, lanes])
        o_ref[:, lanes] = y.astype(BF16)


def _pool_mixer(mix, w_map_b, pool_scale, tt):
    bsz, seq, _ = mix.shape
    pw = len(POOL_WINDOWS) * LANES
    nh = tt // SUBLANES
    last = seq // SUBLANES - 1
    return pl.pallas_call(
        functools.partial(_pool_kernel, seq=seq, tt=tt),
        grid=(bsz, seq // tt),
        in_specs=[
            pl.BlockSpec((None, SUBLANES, pw), lambda b, i: (b, jnp.maximum(i * nh - 1, 0), 0)),
            pl.BlockSpec((None, tt, pw), lambda b, i: (b, i, 0)),
            pl.BlockSpec((None, SUBLANES, pw), lambda b, i: (b, jnp.minimum((i + 1) * nh, last), 0)),
            pl.BlockSpec((None, tt, pw), lambda b, i: (b, i, 1)),
            pl.BlockSpec((len(POOL_WINDOWS), LANES, LANES), lambda b, i: (0, 0, 0)),
            pl.BlockSpec((1, pw), lambda b, i: (0, 0)),
        ],
        out_specs=pl.BlockSpec((None, tt, pw), lambda b, i: (b, i, 0)),
        out_shape=jax.ShapeDtypeStruct((bsz, seq, pw), BF16),
        scratch_shapes=[pltpu.VMEM((tt + 2 * SUBLANES, pw), F32)],
        compiler_params=_params(("arbitrary", "arbitrary")),
        name="pool_mixer",
    )(mix, mix, mix, mix, w_map_b, pool_scale)


def _stack_heads(x):
    return jnp.concatenate([x[:, g * HEAD_DIM:(g + 1) * HEAD_DIM] for g in range(GQA_G)], axis=0)


def _unstack_heads(x, rows):
    return jnp.concatenate([x[g * rows:(g + 1) * rows, :] for g in range(GQA_G)], axis=1)


def _sink_column(sink_ref, kv, rows):
    head = lax.broadcasted_iota(jnp.int32, (GQA_G * rows, 1), 0) // rows
    col = jnp.full((GQA_G * rows, 1), sink_ref[kv * GQA_G], F32)
    for g in range(1, GQA_G):
        col = jnp.where(head == g, sink_ref[kv * GQA_G + g], col)
    return col


def _softmax_pv(s, valid, sink, v):
    if valid is not None:
        s = jnp.where(valid, s, -jnp.inf)
    m = jnp.maximum(jnp.max(s, axis=-1, keepdims=True), sink)
    e = jnp.exp(s - m)
    den = jnp.sum(e, axis=-1, keepdims=True) + jnp.exp(sink - m)
    o = jnp.dot(e.astype(BF16), v, preferred_element_type=F32)
    return o / den


def _qk(q, k):
    return lax.dot_general(q, k, (((1,), (1,)), ((), ())), preferred_element_type=F32)


def _ctx_attn_kernel(sink_ref, q_ref, k_ref, v_ref, z_ref, o_ref, *, seq):
    kv = pl.program_id(1)
    scale = HEAD_DIM ** -0.5
    q = _stack_heads(q_ref[...]).astype(BF16)
    s = _qk(q, k_ref[...].astype(BF16)) * scale
    o = _softmax_pv(s, None, _sink_column(sink_ref, kv, seq), v_ref[...].astype(BF16))
    o_ref[...] = (_unstack_heads(o, seq) * _silu(z_ref[...])).astype(BF16)


def _context_attention(mix, sink, col):
    bsz, seq, _ = mix.shape
    gw = GQA_G * HEAD_DIM
    return pl.pallas_call(
        functools.partial(_ctx_attn_kernel, seq=seq),
        grid=(bsz, N_KV),
        in_specs=[
            pl.BlockSpec(memory_space=pltpu.SMEM),
            pl.BlockSpec((None, seq, gw), lambda b, kv: (b, 0, col["q"] // gw + kv)),
            pl.BlockSpec((None, seq, HEAD_DIM), lambda b, kv: (b, 0, col["k"] // HEAD_DIM + kv)),
            pl.BlockSpec((None, seq, HEAD_DIM), lambda b, kv: (b, 0, col["v"] // HEAD_DIM + kv)),
            pl.BlockSpec((None, seq, gw), lambda b, kv: (b, 0, col["z_attn"] // gw + kv)),
        ],
        out_specs=pl.BlockSpec((None, seq, gw), lambda b, kv: (b, 0, kv)),
        out_shape=jax.ShapeDtypeStruct((bsz, seq, N_KV * gw), BF16),
        compiler_params=_params(("arbitrary", "arbitrary")),
        name="ctx_attention",
    )(sink, mix, mix, mix, mix)


def _rope(x, cos, sin):
    lane = lax.broadcasted_iota(jnp.int32, x.shape, 1)
    quarter = HEAD_DIM // 4
    first = (lane % (2 * quarter)) < quarter
    partner = jnp.where(first, pltpu.roll(x, HEAD_DIM - quarter, 1), pltpu.roll(x, quarter, 1))
    return x * cos + partner * sin


def _lat_attn_kernel(sink_ref, q_ref, kp_ref, kc_ref, kn_ref, vp_ref, vc_ref, vn_ref, kx_ref, vx_ref,
                     z_ref, cp_ref, cc_ref, cn_ref, sp_ref, sc_ref, sn_ref, o_ref):
    kv = pl.program_id(1)
    bi = pl.program_id(2)
    nb = pl.num_programs(2)
    scale = HEAD_DIM ** -0.5
    cos_c = cc_ref[...]
    sin_c = sc_ref[...]
    qh = [_rope(q_ref[:, g * HEAD_DIM:(g + 1) * HEAD_DIM], cos_c, sin_c) for g in range(GQA_G)]
    q = jnp.concatenate(qh, axis=0).astype(BF16)
    k = jnp.concatenate([
        _rope(kp_ref[...], cp_ref[...], sp_ref[...]),
        _rope(kc_ref[...], cos_c, sin_c),
        _rope(kn_ref[...], cn_ref[...], sn_ref[...]),
        kx_ref[...],
    ], axis=0).astype(BF16)
    v = jnp.concatenate([vp_ref[...], vc_ref[...], vn_ref[...], vx_ref[...]], axis=0).astype(BF16)
    s = _qk(q, k) * scale
    n_keys = k.shape[0]
    r = lax.broadcasted_iota(jnp.int32, (GQA_G * BLOCK, n_keys), 0) % BLOCK
    c = lax.broadcasted_iota(jnp.int32, (GQA_G * BLOCK, n_keys), 1)
    valid = (c >= r) & (c <= r + 2 * WINDOW)
    valid = valid & ((c >= BLOCK) | (bi > 0)) & ((c < 2 * BLOCK) | (bi < nb - 1))
    valid = valid | (c >= 3 * BLOCK)
    o = _softmax_pv(s, valid, _sink_column(sink_ref, kv, BLOCK), v)
    o_ref[...] = (_unstack_heads(o, BLOCK) * _silu(z_ref[...])).astype(BF16)


def _latent_attention(mix, sink, k_ctx, v_ctx, cos_t, sin_t, col):
    bsz, seq, _ = mix.shape
    tc = k_ctx.shape[1]
    nb = seq // BLOCK
    gw = GQA_G * HEAD_DIM
    kcol = col["k"] // HEAD_DIM
    vcol = col["v"] // HEAD_DIM
    prev = lambda bi: jnp.maximum(bi - 1, 0)
    nxt = lambda bi: jnp.minimum(bi + 1, nb - 1)
    blk = lambda f, c0: pl.BlockSpec((None, BLOCK, HEAD_DIM), lambda b, kv, bi: (b, f(bi), c0 + kv))
    tab = lambda f: pl.BlockSpec((BLOCK, HEAD_DIM), lambda b, kv, bi: (f(bi), 0))
    same = lambda bi: bi
    return pl.pallas_call(
        _lat_attn_kernel,
        grid=(bsz, N_KV, nb),
        in_specs=[
            pl.BlockSpec(memory_space=pltpu.SMEM),
            pl.BlockSpec((None, BLOCK, gw), lambda b, kv, bi: (b, bi, col["q"] // gw + kv)),
            blk(prev, kcol), blk(same, kcol), blk(nxt, kcol),
            blk(prev, vcol), blk(same, vcol), blk(nxt, vcol),
            pl.BlockSpec((None, tc, HEAD_DIM), lambda b, kv, bi: (b, 0, kv)),
            pl.BlockSpec((None, tc, HEAD_DIM), lambda b, kv, bi: (b, 0, kv)),
            pl.BlockSpec((None, BLOCK, gw), lambda b, kv, bi: (b, bi, col["z_attn"] // gw + kv)),
            tab(prev), tab(same), tab(nxt),
            tab(prev), tab(same), tab(nxt),
        ],
        out_specs=pl.BlockSpec((None, BLOCK, gw), lambda b, kv, bi: (b, bi, kv)),
        out_shape=jax.ShapeDtypeStruct((bsz, seq, N_KV * gw), BF16),
        compiler_params=_params(("arbitrary", "arbitrary", "arbitrary")),
        name="latent_attention",
    )(sink, mix, mix, mix, mix, mix, mix, mix, k_ctx, v_ctx, mix,
      cos_t, cos_t, cos_t, sin_t, sin_t, sin_t)


def _rope_tables(seq):
    rows = seq // GRID_W
    row = jnp.repeat(jnp.arange(rows, dtype=F32), GRID_W)
    colp = jnp.tile(jnp.arange(GRID_W, dtype=F32), rows)
    n_freq = HEAD_DIM // 4
    inv = ROPE_BASE ** (-jnp.arange(n_freq, dtype=F32) / n_freq)
    ang_r = row[:, None] * inv[None]
    ang_c = colp[:, None] * inv[None]
    cos_t = jnp.concatenate([jnp.cos(ang_r)] * 2 + [jnp.cos(ang_c)] * 2, axis=-1)
    sin_t = jnp.concatenate([-jnp.sin(ang_r), jnp.sin(ang_r), -jnp.sin(ang_c), jnp.sin(ang_c)], axis=-1)
    return cos_t, sin_t


def _row_bcast(x, r):
    return jnp.broadcast_to(x[r:r + 1, :], x.shape)


def _lru_kernel(u_ref, z_ref, cw_ref, cb_ref, gw_ref, gb_ref, lam_ref, h0_ref, y_ref, hfin_ref,
                up_ref, hf_ref, *, seq, chunk):
    pad = SUBLANES
    groups = chunk // SUBLANES
    n_chunks = seq // chunk
    up_ref[0:pad, :] = jnp.zeros((pad, LANES), F32)
    up_ref[pad:pad + seq, :] = u_ref[...]
    up_ref[pad + seq:2 * pad + seq, :] = jnp.zeros((pad, LANES), F32)
    nlam = -lam_ref[...]
    softplus = jnp.maximum(nlam, 0.0) + jnp.log1p(jnp.exp(-jnp.abs(nlam)))
    sub = lax.broadcasted_iota(jnp.int32, (chunk, LANES), 0) % SUBLANES

    def gates(t0, d):
        xw = up_ref[pl.ds(pl.multiple_of(t0, SUBLANES), chunk + 2 * pad), :]
        n = chunk + 2 * pad
        xc = cb_ref[...]
        for j in range(CONV_W):
            off = pad - CONV_LEFT + j
            xc = xc + pltpu.roll(xw, (n - off) % n, 0)[0:chunk, :] * cw_ref[j:j + 1, :]
        xb = xc.astype(BF16)
        g_r = jnp.dot(xb, gw_ref[d, 0], preferred_element_type=F32) + gb_ref[2 * d:2 * d + 1, :]
        g_i = jnp.dot(xb, gw_ref[d, 1], preferred_element_type=F32) + gb_ref[2 * d + 1:2 * d + 2, :]
        log_a = (-LRU_C * _sigmoid(g_r)) * softplus[d:d + 1, :]
        a = jnp.exp(log_a)
        one_minus_a2 = -jnp.tanh(log_a) * (a * a + 1.0)
        b = jnp.sqrt(jnp.maximum(one_minus_a2, 0.0)) * (_sigmoid(g_i) * xc)
        return a, b

    def scan_chunk(a, b, carry, reverse):
        for d in (1, 2, 4):
            if reverse:
                keep = sub < SUBLANES - d
                a_sh = jnp.where(keep, pltpu.roll(a, chunk - d, 0), 1.0)
                b_sh = jnp.where(keep, pltpu.roll(b, chunk - d, 0), 0.0)
            else:
                keep = sub >= d
                a_sh = jnp.where(keep, pltpu.roll(a, d, 0), 1.0)
                b_sh = jnp.where(keep, pltpu.roll(b, d, 0), 0.0)
            b = a * b_sh + b
            a = a * a_sh
        edge = 0 if reverse else SUBLANES - 1
        hs = [None] * groups
        order = range(groups - 1, -1, -1) if reverse else range(groups)
        for g in order:
            ag = a[g * SUBLANES:(g + 1) * SUBLANES, :]
            bg = b[g * SUBLANES:(g + 1) * SUBLANES, :]
            hs[g] = ag * carry + bg
            carry = _row_bcast(ag, edge) * carry + _row_bcast(bg, edge)
        return jnp.concatenate(hs, axis=0), carry

    def fwd_body(ci, carry):
        t0 = ci * chunk
        a, b = gates(t0, 0)
        h, carry = scan_chunk(a, b, carry, False)
        hf_ref[pl.ds(pl.multiple_of(t0, SUBLANES), chunk), :] = h
        return carry

    def bwd_body(ci, carry):
        t0 = (n_chunks - 1 - ci) * chunk
        a, b = gates(t0, 1)
        h, carry = scan_chunk(a, b, carry, True)
        rows = pl.ds(pl.multiple_of(t0, SUBLANES), chunk)
        y_ref[rows, :] = ((hf_ref[rows, :] + h) * _silu(z_ref[rows, :])).astype(BF16)
        return carry

    h0 = h0_ref[...]
    c_f = lax.fori_loop(0, n_chunks, fwd_body, jnp.broadcast_to(h0[0:1, :], (SUBLANES, LANES)))
    c_b = lax.fori_loop(0, n_chunks, bwd_body, jnp.broadcast_to(h0[1:2, :], (SUBLANES, LANES)))
    hfin_ref[0:1, :] = c_f[0:1, :]
    hfin_ref[1:2, :] = c_b[0:1, :]


def _rglru(mix, conv_w, conv_b, gate_w_b, gate_b, lam, h0, col, chunk):
    bsz, seq, _ = mix.shape
    lw = LRU_BLOCKS * LANES
    ucol = col["u_lru"] // LANES
    zcol = col["z_lru"] // LANES
    return pl.pallas_call(
        functools.partial(_lru_kernel, seq=seq, chunk=chunk),
        grid=(bsz, LRU_BLOCKS),
        in_specs=[
            pl.BlockSpec((None, seq, LANES), lambda b, n: (b, 0, ucol + n)),
            pl.BlockSpec((None, seq, LANES), lambda b, n: (b, 0, zcol + n)),
            pl.BlockSpec((CONV_W, LANES), lambda b, n: (0, n)),
            pl.BlockSpec((1, LANES), lambda b, n: (0, n)),
            pl.BlockSpec((2, 2, None, LANES, LANES), lambda b, n: (0, 0, n, 0, 0)),
            pl.BlockSpec((4, LANES), lambda b, n: (0, n)),
            pl.BlockSpec((2, LANES), lambda b, n: (0, n)),
            pl.BlockSpec((None, 2, LANES), lambda b, n: (b, 0, n)),
        ],
        out_specs=[
            pl.BlockSpec((None, seq, LANES), lambda b, n: (b, 0, n)),
            pl.BlockSpec((None, 2, LANES), lambda b, n: (b, 0, n)),
        ],
        out_shape=[
            jax.ShapeDtypeStruct((bsz, seq, lw), BF16),
            jax.ShapeDtypeStruct((bsz, 2, lw), F32),
        ],
        scratch_shapes=[
            pltpu.VMEM((seq + 2 * SUBLANES, LANES), F32),
            pltpu.VMEM((seq, LANES), F32),
        ],
        compiler_params=_params(("arbitrary", "arbitrary")),
        name="rglru",
    )(mix, mix, conv_w, conv_b, gate_w_b, gate_b, lam, h0)


def _out_kernel(x_ref, mod_ref, gpre_ref, gpost_ref, yp_ref, ya_ref, yl_ref,
                wm0_ref, wm1_ref, wm2_ref, wp_ref, wa_ref, wl_ref, wo_ref, o_ref, h_ref, acc_ref):
    j = pl.program_id(1)

    @pl.when(j == 0)
    def _():
        h_ref[...] = _prenorm(x_ref[...], gpre_ref[...], mod_ref[0], mod_ref[1]).astype(BF16)
        acc_ref[...] = jnp.zeros_like(acc_ref)

    h = h_ref[...]
    merged = None
    for wm_ref, y_ref, wb_ref in ((wm0_ref, yp_ref, wp_ref), (wm1_ref, ya_ref, wa_ref),
                                  (wm2_ref, yl_ref, wl_ref)):
        gate = _sigmoid(jnp.dot(h, wm_ref[...], preferred_element_type=F32))
        term = gate * jnp.dot(y_ref[...], wb_ref[...], preferred_element_type=F32)
        merged = term if merged is None else merged + term
    acc_ref[...] += jnp.dot(merged.astype(BF16), wo_ref[...], preferred_element_type=F32)

    @pl.when(j == pl.num_programs(1) - 1)
    def _():
        out = acc_ref[...]
        y = out * lax.rsqrt(jnp.mean(out * out, axis=-1, keepdims=True) + EPS) * gpost_ref[...]
        o_ref[...] = x_ref[...] + mod_ref[2] * y


def _out_stage(x2, mod, g_pre, g_post, y_pool, y_attn, y_lru, w_in_b, w_pool_o, w_attn_o, w_lru_o,
               w_out, n_mix, seq, mod_row0, tm, tn):
    rows, d = x2.shape
    if mod_row0 is None:
        mod_map = lambda i, j: ((i * tm) // seq, 0, 0, 0)
    else:
        mod_map = lambda i, j: (mod_row0, 0, 0, 0)
    mcol = lambda br: (lambda i, j: (0, (n_mix + br * d) // tn + j))
    return pl.pallas_call(
        _out_kernel,
        grid=(rows // tm, d // tn),
        in_specs=[
            pl.BlockSpec((tm, d), lambda i, j: (i, 0)),
            pl.BlockSpec((None, 3, 1, d), mod_map),
            pl.BlockSpec((1, d), lambda i, j: (0, 0)),
            pl.BlockSpec((1, d), lambda i, j: (0, 0)),
            pl.BlockSpec((tm, y_pool.shape[1]), lambda i, j: (i, 0)),
            pl.BlockSpec((tm, y_attn.shape[1]), lambda i, j: (i, 0)),
            pl.BlockSpec((tm, y_lru.shape[1]), lambda i, j: (i, 0)),
            pl.BlockSpec((d, tn), mcol(0)),
            pl.BlockSpec((d, tn), mcol(1)),
            pl.BlockSpec((d, tn), mcol(2)),
            pl.BlockSpec((w_pool_o.shape[0], tn), lambda i, j: (0, j)),
            pl.BlockSpec((w_attn_o.shape[0], tn), lambda i, j: (0, j)),
            pl.BlockSpec((w_lru_o.shape[0], tn), lambda i, j: (0, j)),
            pl.BlockSpec((tn, d), lambda i, j: (j, 0)),
        ],
        out_specs=pl.BlockSpec((tm, d), lambda i, j: (i, 0)),
        out_shape=jax.ShapeDtypeStruct((rows, d), F32),
        scratch_shapes=[pltpu.VMEM((tm, d), BF16), pltpu.VMEM((tm, d), F32)],
        compiler_params=_params(("arbitrary", "arbitrary")),
        name="out_stage",
    )(x2, mod, g_pre, g_post, y_pool, y_attn, y_lru, w_in_b, w_in_b, w_in_b,
      w_pool_o, w_attn_o, w_lru_o, w_out)


def _tile(n, want):
    t = min(n, want)
    assert n % t == 0, (n, t)
    return t


def _trunk_layer(x, mod, mod_row0, P, col, n_mix, ctx):
    bsz, seq, d = x.shape
    x2 = x.reshape(bsz * seq, d)
    rows = bsz * seq
    tm_in = _tile(rows if mod_row0 is not None else seq, 1024)
    mix = _in_projection(x2, mod, P["g_pre"], P["w_in"], n_mix, seq, mod_row0, tm_in, 512)
    mix = mix.reshape(bsz, seq, n_mix)
    y_pool = _pool_mixer(mix, P["w_pool_map"], P["pool_scale"], _tile(seq, 512))
    if ctx is None:
        y_attn = _context_attention(mix, P["attn_sink"], col)
        h0 = jnp.zeros((bsz, 2, LRU_BLOCKS * LANES), F32)
    else:
        k_ctx, v_ctx, h0, cos_t, sin_t = ctx
        y_attn = _latent_attention(mix, P["attn_sink"], k_ctx, v_ctx, cos_t, sin_t, col)
    y_lru, h_fin = _rglru(mix, P["lru_conv_w"], P["lru_conv_b"], P["lru_gate_w"], P["lru_gate_b"],
                          P["lru_lambda"], h0, col, _tile(seq, 128))
    tm_out = _tile(rows if mod_row0 is not None else seq, 512)
    x_new = _out_stage(x2, mod, P["g_pre"], P["g_post"], y_pool.reshape(rows, -1),
                       y_attn.reshape(rows, -1), y_lru.reshape(rows, -1), P["w_in"], P["w_pool_o"],
                       P["w_attn_o"], P["w_lru_o"], P["w_out"], n_mix, seq, mod_row0, tm_out, 256)
    return x_new.reshape(bsz, seq, d), mix, h_fin


def kernel(x_prompt, x_sample, cache_k, cache_v, state_lru, c, c_ctx, g_pre, g_post, w_ada, b_ada,
           w_in, w_pool_map, pool_scale, attn_sink, lru_conv_w, lru_conv_b, lru_gate_w, lru_gate_b,
           lru_lambda, w_pool_o, w_attn_o, w_lru_o, w_out):
    depth = w_in.shape[0]
    d = x_prompt.shape[-1]
    dec_b, dec_seq, _ = x_sample.shape
    pool_w = len(POOL_WINDOWS) * LANES
    attn_w = N_HEADS * HEAD_DIM
    kv_w = N_KV * HEAD_DIM
    lru_w = LRU_BLOCKS * LANES
    widths = (("u_pool", pool_w), ("z_pool", pool_w), ("q", attn_w), ("k", kv_w), ("v", kv_w),
              ("z_attn", attn_w), ("u_lru", lru_w), ("z_lru", lru_w))
    col, n_mix = {}, 0
    for name, w in widths:
        col[name] = n_mix
        n_mix += w
    assert w_in.shape[-1] == n_mix + N_BRANCH * d
    assert dec_b < MOD_ROWS

    cvec = jnp.zeros((MOD_ROWS, d), F32).at[:dec_b].set(c).at[dec_b].set(c_ctx)
    mod = _modulation(cvec, w_ada, b_ada)
    mod = mod.reshape(depth, MOD_ROWS, 3, 1, d)
    cos_t, sin_t = _rope_tables(dec_seq)
    past = cache_k.shape[2]
    cache_k2 = cache_k.reshape(dec_b, depth, past, kv_w)
    cache_v2 = cache_v.reshape(dec_b, depth, past, kv_w)

    xp, xs = x_prompt, x_sample
    new_k, new_v, new_h = [], [], []
    for l in range(depth):
        P = {
            "g_pre": g_pre[l][None], "g_post": g_post[l][None],
            "w_in": w_in[l].astype(BF16),
            "w_pool_map": w_pool_map[l].astype(BF16), "pool_scale": pool_scale[l][None],
            "attn_sink": attn_sink[l],
            "lru_conv_w": lru_conv_w[l], "lru_conv_b": lru_conv_b[l][None],
            "lru_gate_w": lru_gate_w[l].astype(BF16), "lru_gate_b": lru_gate_b[l].reshape(4, lru_w),
            "lru_lambda": lru_lambda[l],
            "w_pool_o": w_pool_o[l].astype(BF16), "w_attn_o": w_attn_o[l].astype(BF16),
            "w_lru_o": w_lru_o[l].astype(BF16), "w_out": w_out[l].astype(BF16),
        }
        xp, mix_p, h_fin = _trunk_layer(xp, mod[l], dec_b, P, col, n_mix, None)
        bsz, seq = xp.shape[:2]
        new_k.append(mix_p[:, :, col["k"]:col["k"] + kv_w].reshape(bsz, seq, N_KV, HEAD_DIM))
        new_v.append(mix_p[:, :, col["v"]:col["v"] + kv_w].reshape(bsz, seq, N_KV, HEAD_DIM))
        new_h.append(h_fin)
        ctx = (cache_k2[:, l], cache_v2[:, l], state_lru[:, l], cos_t, sin_t)
        xs, _, _ = _trunk_layer(xs, mod[l], None, P, col, n_mix, ctx)
    return (xp, xs, jnp.stack(new_k, axis=1), jnp.stack(new_v, axis=1), jnp.stack(new_h, axis=1))
```

```python
import functools

import jax
import jax.numpy as jnp
from jax import lax
from jax.experimental import pallas as pl
from jax.experimental.pallas import tpu as pltpu

F32 = jnp.float32
BF16 = jnp.bfloat16

EPS = 1e-6
LANES = 128
SUBLANES = 8
POOL_WINDOWS = (2, 4, 8, 16)
N_HEADS = 8
N_KV = 2
HEAD_DIM = 128
GQA_G = N_HEADS // N_KV
WINDOW = 128
BLOCK = 128
GRID_W = 64
ROPE_BASE = 10000.0
LRU_BLOCKS = 4
CONV_W = 4
CONV_LEFT = 2
LRU_C = 8.0
N_BRANCH = 3
MOD_ROWS = 16

VMEM_LIMIT = 56 * 1024 * 1024


def _sigmoid(x):
    return 0.5 * (jnp.tanh(0.5 * x) + 1.0)


def _silu(x):
    return x * _sigmoid(x)


def _params(sem):
    return pltpu.CompilerParams(dimension_semantics=sem, vmem_limit_bytes=VMEM_LIMIT)


def _mod_kernel(c_ref, w_ref, b_ref, o_ref):
    s = _silu(c_ref[...]).astype(BF16)
    o_ref[...] = jnp.dot(s, w_ref[...].astype(BF16), preferred_element_type=F32) + b_ref[...]


def _modulation(cvec, w_ada, b_ada):
    depth, d, n = w_ada.shape
    tn = 768
    return pl.pallas_call(
        _mod_kernel,
        grid=(depth, n // tn),
        in_specs=[
            pl.BlockSpec((MOD_ROWS, d), lambda l, j: (0, 0)),
            pl.BlockSpec((None, d, tn), lambda l, j: (l, 0, j)),
            pl.BlockSpec((None, 1, tn), lambda l, j: (l, 0, j)),
        ],
        out_specs=pl.BlockSpec((None, MOD_ROWS, tn), lambda l, j: (l, 0, j)),
        out_shape=jax.ShapeDtypeStruct((depth, MOD_ROWS, n), F32),
        compiler_params=_params(("arbitrary", "arbitrary")),
        name="adaln_mod",
    )(cvec, w_ada, b_ada.reshape(depth, 1, n))


NORM_ROWS = 16
NORM_UNROLL = 8


def _prenorm_into(h_ref, x_ref, g_ref, mod_ref):
    gain = g_ref[...] * (1.0 + mod_ref[1])
    shift = mod_ref[0]

    def body(c, carry):
        rows = pl.ds(pl.multiple_of(c * NORM_ROWS, NORM_ROWS), NORM_ROWS)
        x = x_ref[rows, :]
        inv = lax.rsqrt(jnp.mean(x * x, axis=-1, keepdims=True) + EPS)
        h_ref[rows, :] = ((x * inv) * gain + shift).astype(BF16)
        return carry

    lax.fori_loop(0, x_ref.shape[0] // NORM_ROWS, body, 0, unroll=NORM_UNROLL)


def _inproj_kernel(x_ref, mod_ref, g_ref, w_ref, o_ref, h_ref):
    @pl.when(pl.program_id(1) == 0)
    def _():
        _prenorm_into(h_ref, x_ref, g_ref, mod_ref)

    o_ref[...] = jnp.dot(h_ref[...], w_ref[...], preferred_element_type=F32)


def _in_projection(x2, mod, g_pre, w_in_b, n_mix, seq, mod_row0, tm, tn):
    rows, d = x2.shape
    if mod_row0 is None:
        mod_map = lambda i, j: ((i * tm) // seq, 0, 0, 0)
    else:
        mod_map = lambda i, j: (mod_row0, 0, 0, 0)
    return pl.pallas_call(
        _inproj_kernel,
        grid=(rows // tm, n_mix // tn),
        in_specs=[
            pl.BlockSpec((tm, d), lambda i, j: (i, 0)),
            pl.BlockSpec((None, 3, 1, d), mod_map),
            pl.BlockSpec((1, d), lambda i, j: (0, 0)),
            pl.BlockSpec((d, tn), lambda i, j: (0, j)),
        ],
        out_specs=pl.BlockSpec((tm, tn), lambda i, j: (i, j)),
        out_shape=jax.ShapeDtypeStruct((rows, n_mix), F32),
        scratch_shapes=[pltpu.VMEM((tm, d), BF16)],
        compiler_params=_params(("arbitrary", "arbitrary")),
        name="in_proj",
    )(x2, mod, g_pre, w_in_b)


def _pool_kernel(up_ref, uc_ref, un_ref, z_ref, wm_ref, sc_ref, o_ref, ext_ref, *, seq, tt):
    i = pl.program_id(1)
    nt = pl.num_programs(1)
    halo = SUBLANES
    ext_ref[0:halo, :] = jnp.where(i > 0, up_ref[...], 0.0)
    ext_ref[halo:halo + tt, :] = uc_ref[...]
    ext_ref[halo + tt:2 * halo + tt, :] = jnp.where(i < nt - 1, un_ref[...], 0.0)
    t = i * tt + lax.broadcasted_iota(jnp.int32, (tt, 1), 0)
    for g, w in enumerate(POOL_WINDOWS):
        lanes = slice(g * LANES, (g + 1) * LANES)
        base = halo - w // 2
        acc = ext_ref[base:base + tt, lanes]
        for k in range(1, w):
            acc = acc + ext_ref[base + k:base + k + tt, lanes]
        cnt = (jnp.minimum(t + w // 2, seq) - jnp.maximum(t - w // 2, 0)).astype(F32)
        dlt = acc / cnt - uc_ref[:, lanes]
        y = jnp.dot(dlt.astype(BF16), wm_ref[g], preferred_element_type=F32)
        y = (y * sc_ref[:, lanes]) * _silu(z_ref[:, lanes])
        o_ref[:, lanes] = y.astype(BF16)


def _pool_mixer(mix, w_map_b, pool_scale, tt):
    bsz, seq, _ = mix.shape
    pw = len(POOL_WINDOWS) * LANES
    nh = tt // SUBLANES
    last = seq // SUBLANES - 1
    return pl.pallas_call(
        functools.partial(_pool_kernel, seq=seq, tt=tt),
        grid=(bsz, seq // tt),
        in_specs=[
            pl.BlockSpec((None, SUBLANES, pw), lambda b, i: (b, jnp.maximum(i * nh - 1, 0), 0)),
            pl.BlockSpec((None, tt, pw), lambda b, i: (b, i, 0)),
            pl.BlockSpec((None, SUBLANES, pw), lambda b, i: (b, jnp.minimum((i + 1) * nh, last), 0)),
            pl.BlockSpec((None, tt, pw), lambda b, i: (b, i, 1)),
            pl.BlockSpec((len(POOL_WINDOWS), LANES, LANES), lambda b, i: (0, 0, 0)),
            pl.BlockSpec((1, pw), lambda b, i: (0, 0)),
        ],
        out_specs=pl.BlockSpec((None, tt, pw), lambda b, i: (b, i, 0)),
        out_shape=jax.ShapeDtypeStruct((bsz, seq, pw), BF16),
        scratch_shapes=[pltpu.VMEM((tt + 2 * SUBLANES, pw), F32)],
        compiler_params=_params(("arbitrary", "arbitrary")),
        name="pool_mixer",
    )(mix, mix, mix, mix, w_map_b, pool_scale)


def _stack_heads(x):
    return jnp.concatenate([x[:, g * HEAD_DIM:(g + 1) * HEAD_DIM] for g in range(GQA_G)], axis=0)


def _unstack_heads(x, rows):
    return jnp.concatenate([x[g * rows:(g + 1) * rows, :] for g in range(GQA_G)], axis=1)


def _sink_column(sink_ref, kv, rows):
    head = lax.broadcasted_iota(jnp.int32, (GQA_G * rows, 1), 0) // rows
    col = jnp.full((GQA_G * rows, 1), sink_ref[kv * GQA_G], F32)
    for g in range(1, GQA_G):
        col = jnp.where(head == g, sink_ref[kv * GQA_G + g], col)
    return col


def _softmax_pv(s, valid, sink, v):
    if valid is not None:
        s = jnp.where(valid, s, -jnp.inf)
    m = jnp.maximum(jnp.max(s, axis=-1, keepdims=True), sink)
    e = jnp.exp(s - m)
    den = jnp.sum(e, axis=-1, keepdims=True) + jnp.exp(sink - m)
    o = jnp.dot(e.astype(BF16), v, preferred_element_type=F32)
    return o / den


def _qk(q, k):
    return lax.dot_general(q, k, (((1,), (1,)), ((), ())), preferred_element_type=F32)


def _ctx_attn_kernel(sink_ref, q_ref, k_ref, v_ref, z_ref, o_ref, *, seq):
    kv = pl.program_id(1)
    scale = HEAD_DIM ** -0.5
    q = _stack_heads(q_ref[...]).astype(BF16)
    s = _qk(q, k_ref[...].astype(BF16)) * scale
    o = _softmax_pv(s, None, _sink_column(sink_ref, kv, seq), v_ref[...].astype(BF16))
    o_ref[...] = (_unstack_heads(o, seq) * _silu(z_ref[...])).astype(BF16)


def _context_attention(mix, sink, col):
    bsz, seq, _ = mix.shape
    gw = GQA_G * HEAD_DIM
    return pl.pallas_call(
        functools.partial(_ctx_attn_kernel, seq=seq),
        grid=(bsz, N_KV),
        in_specs=[
            pl.BlockSpec(memory_space=pltpu.SMEM),
            pl.BlockSpec((None, seq, gw), lambda b, kv: (b, 0, col["q"] // gw + kv)),
            pl.BlockSpec((None, seq, HEAD_DIM), lambda b, kv: (b, 0, col["k"] // HEAD_DIM + kv)),
            pl.BlockSpec((None, seq, HEAD_DIM), lambda b, kv: (b, 0, col["v"] // HEAD_DIM + kv)),
            pl.BlockSpec((None, seq, gw), lambda b, kv: (b, 0, col["z_attn"] // gw + kv)),
        ],
        out_specs=pl.BlockSpec((None, seq, gw), lambda b, kv: (b, 0, kv)),
        out_shape=jax.ShapeDtypeStruct((bsz, seq, N_KV * gw), BF16),
        compiler_params=_params(("arbitrary", "arbitrary")),
        name="ctx_attention",
    )(sink, mix, mix, mix, mix)


def _rope(x, cos, sin):
    lane = lax.broadcasted_iota(jnp.int32, x.shape, 1)
    quarter = HEAD_DIM // 4
    first = (lane % (2 * quarter)) < quarter
    partner = jnp.where(first, pltpu.roll(x, HEAD_DIM - quarter, 1), pltpu.roll(x, quarter, 1))
    return x * cos + partner * sin


def _lat_attn_kernel(sink_ref, q_ref, k_ref, v_ref, kx_ref, vx_ref, z_ref, cos_ref, sin_ref, o_ref,
                     kr_ref, vb_ref, bias_ref, *, seq, qc):
    kv = pl.program_id(1)
    ci = pl.program_id(2)
    nb = seq // BLOCK
    nb_c = qc // BLOCK
    span = 3 * BLOCK
    scale = HEAD_DIM ** -0.5

    @pl.when(ci == 0)
    def _():
        zeros = jnp.zeros((BLOCK, HEAD_DIM), BF16)
        for ref in (kr_ref, vb_ref):
            ref[0:BLOCK, :] = zeros
            ref[BLOCK + seq:2 * BLOCK + seq, :] = zeros

        def fill(i, carry):
            rows = pl.ds(pl.multiple_of(i * BLOCK, BLOCK), BLOCK)
            dst = pl.ds(pl.multiple_of((i + 1) * BLOCK, BLOCK), BLOCK)
            kr_ref[dst, :] = _rope(k_ref[rows, :], cos_ref[rows, :], sin_ref[rows, :]).astype(BF16)
            vb_ref[dst, :] = v_ref[rows, :].astype(BF16)
            return carry

        lax.fori_loop(0, nb, fill, 0, unroll=4)
        r = lax.broadcasted_iota(jnp.int32, (GQA_G * BLOCK, span), 0) % BLOCK
        c = lax.broadcasted_iota(jnp.int32, (GQA_G * BLOCK, span), 1)
        band = (c >= r) & (c <= r + 2 * WINDOW)
        neg = jnp.float32(-jnp.inf)
        bias_ref[0] = jnp.where(band & (c >= BLOCK), 0.0, neg)
        bias_ref[1] = jnp.where(band, 0.0, neg)
        bias_ref[2] = jnp.where(band & (c < 2 * BLOCK), 0.0, neg)

    sink = _sink_column(sink_ref, kv, BLOCK)
    kx = kx_ref[...].astype(BF16)
    vx = vx_ref[...].astype(BF16)

    def block(jb, carry):
        bi = ci * nb_c + jb
        rows = pl.ds(pl.multiple_of(jb * BLOCK, BLOCK), BLOCK)
        pos = pl.ds(pl.multiple_of(bi * BLOCK, BLOCK), BLOCK)
        cos_q = cos_ref[pos, :] * scale
        sin_q = sin_ref[pos, :] * scale
        q = jnp.concatenate(
            [_rope(q_ref[rows, g * HEAD_DIM:(g + 1) * HEAD_DIM], cos_q, sin_q) for g in range(GQA_G)],
            axis=0).astype(BF16)
        win = pl.ds(pl.multiple_of(bi * BLOCK, BLOCK), span)
        variant = jnp.where(bi == 0, 0, jnp.where(bi == nb - 1, 2, 1))
        s_w = _qk(q, kr_ref[win, :]) + bias_ref[variant]
        s_c = _qk(q, kx)
        m = jnp.maximum(jnp.maximum(jnp.max(s_w, axis=-1, keepdims=True),
                                    jnp.max(s_c, axis=-1, keepdims=True)), sink)
        e_w = jnp.exp(s_w - m)
        e_c = jnp.exp(s_c - m)
        den = (jnp.sum(e_w, axis=-1, keepdims=True) + jnp.sum(e_c, axis=-1, keepdims=True)
               + jnp.exp(sink - m))
        o = (jnp.dot(e_w.astype(BF16), vb_ref[win, :], preferred_element_type=F32)
             + jnp.dot(e_c.astype(BF16), vx, preferred_element_type=F32)) / den
        o_ref[rows, :] = (_unstack_heads(o, BLOCK) * _silu(z_ref[rows, :])).astype(BF16)
        return carry

    lax.fori_loop(0, nb_c, block, 0, unroll=2)


def _latent_attention(mix, sink, k_ctx, v_ctx, cos_t, sin_t, col):
    bsz, seq, _ = mix.shape
    tc = k_ctx.shape[1]
    assert seq // BLOCK >= 2
    qc = _tile(seq, 1024)
    gw = GQA_G * HEAD_DIM
    kcol = col["k"] // HEAD_DIM
    vcol = col["v"] // HEAD_DIM
    return pl.pallas_call(
        functools.partial(_lat_attn_kernel, seq=seq, qc=qc),
        grid=(bsz, N_KV, seq // qc),
        in_specs=[
            pl.BlockSpec(memory_space=pltpu.SMEM),
            pl.BlockSpec((None, qc, gw), lambda b, kv, ci: (b, ci, col["q"] // gw + kv)),
            pl.BlockSpec((None, seq, HEAD_DIM), lambda b, kv, ci: (b, 0, kcol + kv)),
            pl.BlockSpec((None, seq, HEAD_DIM), lambda b, kv, ci: (b, 0, vcol + kv)),
            pl.BlockSpec((None, tc, HEAD_DIM), lambda b, kv, ci: (b, 0, kv)),
            pl.BlockSpec((None, tc, HEAD_DIM), lambda b, kv, ci: (b, 0, kv)),
            pl.BlockSpec((None, qc, gw), lambda b, kv, ci: (b, ci, col["z_attn"] // gw + kv)),
            pl.BlockSpec((seq, HEAD_DIM), lambda b, kv, ci: (0, 0)),
            pl.BlockSpec((seq, HEAD_DIM), lambda b, kv, ci: (0, 0)),
        ],
        out_specs=pl.BlockSpec((None, qc, gw), lambda b, kv, ci: (b, ci, kv)),
        out_shape=jax.ShapeDtypeStruct((bsz, seq, N_KV * gw), BF16),
        scratch_shapes=[
            pltpu.VMEM((seq + 2 * BLOCK, HEAD_DIM), BF16),
            pltpu.VMEM((seq + 2 * BLOCK, HEAD_DIM), BF16),
            pltpu.VMEM((3, GQA_G * BLOCK, 3 * BLOCK), F32),
        ],
        compiler_params=_params(("arbitrary", "arbitrary", "arbitrary")),
        name="latent_attention",
    )(sink, mix, mix, mix, k_ctx, v_ctx, mix, cos_t, sin_t)


def _rope_tables(seq):
    rows = seq // GRID_W
    row = jnp.repeat(jnp.arange(rows, dtype=F32), GRID_W)
    colp = jnp.tile(jnp.arange(GRID_W, dtype=F32), rows)
    n_freq = HEAD_DIM // 4
    inv = ROPE_BASE ** (-jnp.arange(n_freq, dtype=F32) / n_freq)
    ang_r = row[:, None] * inv[None]
    ang_c = colp[:, None] * inv[None]
    cos_t = jnp.concatenate([jnp.cos(ang_r)] * 2 + [jnp.cos(ang_c)] * 2, axis=-1)
    sin_t = jnp.concatenate([-jnp.sin(ang_r), jnp.sin(ang_r), -jnp.sin(ang_c), jnp.sin(ang_c)], axis=-1)
    return cos_t, sin_t


def _row_bcast(x, r):
    return jnp.broadcast_to(x[r:r + 1, :], x.shape)


SCAN_ROWS = SUBLANES * SUBLANES


def _segment_scan(a_ref, b_ref, t0, carry, reverse):
    idx = [pl.ds(t0 + k, SUBLANES, stride=SUBLANES) for k in range(SUBLANES)]
    a = [a_ref[i, :] for i in idx]
    b = [b_ref[i, :] for i in idx]
    order = list(range(SUBLANES - 1, -1, -1)) if reverse else list(range(SUBLANES))
    h = [None] * SUBLANES
    p = [None] * SUBLANES
    prev = None
    for k in order:
        h[k] = b[k] if prev is None else a[k] * h[prev] + b[k]
        p[k] = a[k] if prev is None else a[k] * p[prev]
        prev = k
    pp, hh = p[prev], h[prev]
    sub = lax.broadcasted_iota(jnp.int32, (SUBLANES, LANES), 0)
    for d in (1, 2, 4):
        keep = (sub < SUBLANES - d) if reverse else (sub >= d)
        shift = SUBLANES - d if reverse else d
        p_sh = jnp.where(keep, pltpu.roll(pp, shift, 0), 1.0)
        h_sh = jnp.where(keep, pltpu.roll(hh, shift, 0), 0.0)
        hh = pp * h_sh + hh
        pp = pp * p_sh
    end = pp * carry + hh
    if reverse:
        c_in = jnp.where(sub < SUBLANES - 1, pltpu.roll(end, SUBLANES - 1, 0), carry)
    else:
        c_in = jnp.where(sub >= 1, pltpu.roll(end, 1, 0), carry)
    for k in range(SUBLANES):
        b_ref[idx[k], :] = h[k] + p[k] * c_in
    return _row_bcast(end, 0 if reverse else SUBLANES - 1)


def _lru_kernel(u_ref, z_ref, cw_ref, cb_ref, gw_ref, gb_ref, lam_ref, h0_ref, y_ref, hfin_ref,
                up_ref, af_ref, bf_ref, ab_ref, bb_ref, *, seq, chunk):
    pad = SUBLANES
    up_ref[0:pad, :] = jnp.zeros((pad, LANES), F32)
    up_ref[pad:pad + seq, :] = u_ref[...]
    up_ref[pad + seq:2 * pad + seq, :] = jnp.zeros((pad, LANES), F32)
    nlam = -lam_ref[...]
    softplus = jnp.maximum(nlam, 0.0) + jnp.log1p(jnp.exp(-jnp.abs(nlam)))
    rate = -LRU_C * softplus

    def gates(ci, carry):
        rows = pl.ds(pl.multiple_of(ci * chunk, chunk), chunk)
        n = chunk + 2 * pad
        xw = up_ref[pl.ds(pl.multiple_of(ci * chunk, chunk), n), :]
        xc = cb_ref[...]
        for j in range(CONV_W):
            off = pad - CONV_LEFT + j
            xc = xc + pltpu.roll(xw, (n - off) % n, 0)[0:chunk, :] * cw_ref[j:j + 1, :]
        g = jnp.dot(xc.astype(BF16), gw_ref[...], preferred_element_type=F32) + gb_ref[...]
        for d, (a_ref, b_ref) in enumerate(((af_ref, bf_ref), (ab_ref, bb_ref))):
            g_r = g[:, (2 * d) * LANES:(2 * d + 1) * LANES]
            g_i = g[:, (2 * d + 1) * LANES:(2 * d + 2) * LANES]
            log_a = _sigmoid(g_r) * rate[d:d + 1, :]
            a = jnp.exp(log_a)
            one_minus_a2 = -jnp.tanh(log_a) * (a * a + 1.0)
            a_ref[rows, :] = a
            b_ref[rows, :] = jnp.sqrt(jnp.maximum(one_minus_a2, 0.0)) * (_sigmoid(g_i) * xc)
        return carry

    lax.fori_loop(0, seq // chunk, gates, 0, unroll=2)

    n_scan = seq // SCAN_ROWS

    def scan(ci, carry):
        c_f, c_b = carry
        c_f = _segment_scan(af_ref, bf_ref, ci * SCAN_ROWS, c_f, False)
        c_b = _segment_scan(ab_ref, bb_ref, (n_scan - 1 - ci) * SCAN_ROWS, c_b, True)
        return c_f, c_b

    h0 = h0_ref[...]
    c_f, c_b = lax.fori_loop(0, n_scan, scan, (jnp.broadcast_to(h0[0:1, :], (SUBLANES, LANES)),
                                               jnp.broadcast_to(h0[1:2, :], (SUBLANES, LANES))))
    hfin_ref[0:1, :] = c_f[0:1, :]
    hfin_ref[1:2, :] = c_b[0:1, :]

    def gate_out(ci, carry):
        rows = pl.ds(pl.multiple_of(ci * chunk, chunk), chunk)
        y_ref[rows, :] = ((bf_ref[rows, :] + bb_ref[rows, :]) * _silu(z_ref[rows, :])).astype(BF16)
        return carry

    lax.fori_loop(0, seq // chunk, gate_out, 0, unroll=2)


def _rglru(mix, conv_w, conv_b, gate_w_b, gate_b, lam, h0, col, chunk):
    bsz, seq, _ = mix.shape
    assert seq % SCAN_ROWS == 0 and seq % chunk == 0
    lw = LRU_BLOCKS * LANES
    ucol = col["u_lru"] // LANES
    zcol = col["z_lru"] // LANES
    return pl.pallas_call(
        functools.partial(_lru_kernel, seq=seq, chunk=chunk),
        grid=(bsz, LRU_BLOCKS),
        in_specs=[
            pl.BlockSpec((None, seq, LANES), lambda b, n: (b, 0, ucol + n)),
            pl.BlockSpec((None, seq, LANES), lambda b, n: (b, 0, zcol + n)),
            pl.BlockSpec((CONV_W, LANES), lambda b, n: (0, n)),
            pl.BlockSpec((1, LANES), lambda b, n: (0, n)),
            pl.BlockSpec((None, LANES, 4 * LANES), lambda b, n: (n, 0, 0)),
            pl.BlockSpec((None, 1, 4 * LANES), lambda b, n: (n, 0, 0)),
            pl.BlockSpec((2, LANES), lambda b, n: (0, n)),
            pl.BlockSpec((None, 2, LANES), lambda b, n: (b, 0, n)),
        ],
        out_specs=[
            pl.BlockSpec((None, seq, LANES), lambda b, n: (b, 0, n)),
            pl.BlockSpec((None, 2, LANES), lambda b, n: (b, 0, n)),
        ],
        out_shape=[
            jax.ShapeDtypeStruct((bsz, seq, lw), BF16),
            jax.ShapeDtypeStruct((bsz, 2, lw), F32),
        ],
        scratch_shapes=[pltpu.VMEM((seq + 2 * SUBLANES, LANES), F32)]
        + [pltpu.VMEM((seq, LANES), F32)] * 4,
        compiler_params=_params(("arbitrary", "arbitrary")),
        name="rglru",
    )(mix, mix, conv_w, conv_b, gate_w_b, gate_b, lam, h0)


def _out_kernel(x_ref, mod_ref, gpre_ref, gpost_ref, yp_ref, ya_ref, yl_ref,
                wm0_ref, wm1_ref, wm2_ref, wp_ref, wa_ref, wl_ref, wo_ref, o_ref, h_ref, mg_ref, acc_ref,
                *, n_merge, n_out, tn, tn_out):
    j = pl.program_id(1)

    @pl.when(j == 0)
    def _():
        _prenorm_into(h_ref, x_ref, gpre_ref, mod_ref)

    @pl.when(j < n_merge)
    def _():
        h = h_ref[...]
        merged = None
        for wm_ref, y_ref, wb_ref in ((wm0_ref, yp_ref, wp_ref), (wm1_ref, ya_ref, wa_ref),
                                      (wm2_ref, yl_ref, wl_ref)):
            gate = _sigmoid(jnp.dot(h, wm_ref[...], preferred_element_type=F32))
            term = gate * jnp.dot(y_ref[...], wb_ref[...], preferred_element_type=F32)
            merged = term if merged is None else merged + term
        merged = merged.astype(BF16)
        for c in range(n_merge):
            @pl.when(j == c)
            def _():
                mg_ref[:, c * tn:(c + 1) * tn] = merged

    for c in range(n_out):
        @pl.when(j == n_merge + c)
        def _():
            acc_ref[:, c * tn_out:(c + 1) * tn_out] = jnp.dot(
                mg_ref[...], wo_ref[...], preferred_element_type=F32)

    @pl.when(j == n_merge + n_out - 1)
    def _():
        gain = gpost_ref[...] * mod_ref[2]

        def body(c, carry):
            rows = pl.ds(pl.multiple_of(c * NORM_ROWS, NORM_ROWS), NORM_ROWS)
            out = acc_ref[rows, :]
            inv = lax.rsqrt(jnp.mean(out * out, axis=-1, keepdims=True) + EPS)
            o_ref[rows, :] = x_ref[rows, :] + (out * inv) * gain
            return carry

        lax.fori_loop(0, o_ref.shape[0] // NORM_ROWS, body, 0, unroll=NORM_UNROLL)


def _out_stage(x2, mod, g_pre, g_post, y_pool, y_attn, y_lru, w_in_b, w_pool_o, w_attn_o, w_lru_o,
               w_out, n_mix, seq, mod_row0, tm, tn, tn_out):
    rows, d = x2.shape
    n_merge = d // tn
    n_out = d // tn_out
    if mod_row0 is None:
        mod_map = lambda i, j: ((i * tm) // seq, 0, 0, 0)
    else:
        mod_map = lambda i, j: (mod_row0, 0, 0, 0)
    mstep = lambda j: jnp.minimum(j, n_merge - 1)
    mcol = lambda br: (lambda i, j: (0, (n_mix + br * d) // tn + mstep(j)))
    bcol = lambda i, j: (0, mstep(j))
    return pl.pallas_call(
        functools.partial(_out_kernel, n_merge=n_merge, n_out=n_out, tn=tn, tn_out=tn_out),
        grid=(rows // tm, n_merge + n_out),
        in_specs=[
            pl.BlockSpec((tm, d), lambda i, j: (i, 0)),
            pl.BlockSpec((None, 3, 1, d), mod_map),
            pl.BlockSpec((1, d), lambda i, j: (0, 0)),
            pl.BlockSpec((1, d), lambda i, j: (0, 0)),
            pl.BlockSpec((tm, y_pool.shape[1]), lambda i, j: (i, 0)),
            pl.BlockSpec((tm, y_attn.shape[1]), lambda i, j: (i, 0)),
            pl.BlockSpec((tm, y_lru.shape[1]), lambda i, j: (i, 0)),
            pl.BlockSpec((d, tn), mcol(0)),
            pl.BlockSpec((d, tn), mcol(1)),
            pl.BlockSpec((d, tn), mcol(2)),
            pl.BlockSpec((w_pool_o.shape[0], tn), bcol),
            pl.BlockSpec((w_attn_o.shape[0], tn), bcol),
            pl.BlockSpec((w_lru_o.shape[0], tn), bcol),
            pl.BlockSpec((d, tn_out), lambda i, j: (0, jnp.maximum(j - n_merge, 0))),
        ],
        out_specs=pl.BlockSpec((tm, d), lambda i, j: (i, 0)),
        out_shape=jax.ShapeDtypeStruct((rows, d), F32),
        scratch_shapes=[pltpu.VMEM((tm, d), BF16), pltpu.VMEM((tm, d), BF16),
                        pltpu.VMEM((tm, d), F32)],
        compiler_params=_params(("arbitrary", "arbitrary")),
        name="out_stage",
    )(x2, mod, g_pre, g_post, y_pool, y_attn, y_lru, w_in_b, w_in_b, w_in_b,
      w_pool_o, w_attn_o, w_lru_o, w_out)


def _tile(n, want):
    t = min(n, want)
    assert n % t == 0, (n, t)
    return t


def _trunk_layer(x, mod, mod_row0, P, col, n_mix, ctx):
    bsz, seq, d = x.shape
    x2 = x.reshape(bsz * seq, d)
    rows = bsz * seq
    tm_in = _tile(rows if mod_row0 is not None else seq, 1024)
    mix = _in_projection(x2, mod, P["g_pre"], P["w_in"], n_mix, seq, mod_row0, tm_in, 768)
    mix = mix.reshape(bsz, seq, n_mix)
    y_pool = _pool_mixer(mix, P["w_pool_map"], P["pool_scale"], _tile(seq, 512))
    if ctx is None:
        y_attn = _context_attention(mix, P["attn_sink"], col)
        h0 = jnp.zeros((bsz, 2, LRU_BLOCKS * LANES), F32)
    else:
        k_ctx, v_ctx, h0, cos_t, sin_t = ctx
        y_attn = _latent_attention(mix, P["attn_sink"], k_ctx, v_ctx, cos_t, sin_t, col)
    y_lru, h_fin = _rglru(mix, P["lru_conv_w"], P["lru_conv_b"], P["lru_gate_w"], P["lru_gate_b"],
                          P["lru_lambda"], h0, col, _tile(seq, 128))
    tm_out = _tile(rows if mod_row0 is not None else seq, 512)
    x_new = _out_stage(x2, mod, P["g_pre"], P["g_post"], y_pool.reshape(rows, -1),
                       y_attn.reshape(rows, -1), y_lru.reshape(rows, -1), P["w_in"], P["w_pool_o"],
                       P["w_attn_o"], P["w_lru_o"], P["w_out"], n_mix, seq, mod_row0, tm_out, 256, 512)
    return x_new.reshape(bsz, seq, d), mix, h_fin


def kernel(x_prompt, x_sample, cache_k, cache_v, state_lru, c, c_ctx, g_pre, g_post, w_ada, b_ada,
           w_in, w_pool_map, pool_scale, attn_sink, lru_conv_w, lru_conv_b, lru_gate_w, lru_gate_b,
           lru_lambda, w_pool_o, w_attn_o, w_lru_o, w_out):
    depth = w_in.shape[0]
    d = x_prompt.shape[-1]
    dec_b, dec_seq, _ = x_sample.shape
    pool_w = len(POOL_WINDOWS) * LANES
    attn_w = N_HEADS * HEAD_DIM
    kv_w = N_KV * HEAD_DIM
    lru_w = LRU_BLOCKS * LANES
    widths = (("u_pool", pool_w), ("z_pool", pool_w), ("q", attn_w), ("k", kv_w), ("v", kv_w),
              ("z_attn", attn_w), ("u_lru", lru_w), ("z_lru", lru_w))
    col, n_mix = {}, 0
    for name, w in widths:
        col[name] = n_mix
        n_mix += w
    assert w_in.shape[-1] == n_mix + N_BRANCH * d
    assert dec_b < MOD_ROWS

    cvec = jnp.zeros((MOD_ROWS, d), F32).at[:dec_b].set(c).at[dec_b].set(c_ctx)
    mod = _modulation(cvec, w_ada, b_ada)
    mod = mod.reshape(depth, MOD_ROWS, 3, 1, d)
    cos_t, sin_t = _rope_tables(dec_seq)
    past = cache_k.shape[2]
    cache_k2 = cache_k.reshape(dec_b, depth, past, kv_w)
    cache_v2 = cache_v.reshape(dec_b, depth, past, kv_w)

    xp, xs = x_prompt, x_sample
    new_k, new_v, new_h = [], [], []
    for l in range(depth):
        P = {
            "g_pre": g_pre[l][None], "g_post": g_post[l][None],
            "w_in": w_in[l].astype(BF16),
            "w_pool_map": w_pool_map[l].astype(BF16), "pool_scale": pool_scale[l][None],
            "attn_sink": attn_sink[l],
            "lru_conv_w": lru_conv_w[l], "lru_conv_b": lru_conv_b[l][None],
            "lru_gate_w": lru_gate_w[l].transpose(2, 3, 0, 1, 4).reshape(
                LRU_BLOCKS, LANES, 4 * LANES).astype(BF16),
            "lru_gate_b": lru_gate_b[l].reshape(2, 2, LRU_BLOCKS, LANES).transpose(2, 0, 1, 3).reshape(
                LRU_BLOCKS, 1, 4 * LANES),
            "lru_lambda": lru_lambda[l],
            "w_pool_o": w_pool_o[l].astype(BF16), "w_attn_o": w_attn_o[l].astype(BF16),
            "w_lru_o": w_lru_o[l].astype(BF16), "w_out": w_out[l].astype(BF16),
        }
        xp, mix_p, h_fin = _trunk_layer(xp, mod[l], dec_b, P, col, n_mix, None)
        bsz, seq = xp.shape[:2]
        new_k.append(mix_p[:, :, col["k"]:col["k"] + kv_w].reshape(bsz, seq, N_KV, HEAD_DIM))
        new_v.append(mix_p[:, :, col["v"]:col["v"] + kv_w].reshape(bsz, seq, N_KV, HEAD_DIM))
        new_h.append(h_fin)
        ctx = (cache_k2[:, l], cache_v2[:, l], state_lru[:, l], cos_t, sin_t)
        xs, _, _ = _trunk_layer(xs, mod[l], None, P, col, n_mix, ctx)
    return (xp, xs, jnp.stack(new_k, axis=1), jnp.stack(new_v, axis=1), jnp.stack(new_h, axis=1))
```

```python
import functools

import jax
import jax.numpy as jnp
from jax import lax
from jax.experimental import pallas as pl
from jax.experimental.pallas import tpu as pltpu

F32 = jnp.float32
BF16 = jnp.bfloat16

EPS = 1e-6
LANES = 128
SUBLANES = 8
POOL_WINDOWS = (2, 4, 8, 16)
N_HEADS = 8
N_KV = 2
HEAD_DIM = 128
GQA_G = N_HEADS // N_KV
WINDOW = 128
BLOCK = 128
GRID_W = 64
ROPE_BASE = 10000.0
LRU_BLOCKS = 4
CONV_W = 4
CONV_LEFT = 2
LRU_C = 8.0
N_BRANCH = 3
MOD_ROWS = 16

VMEM_LIMIT = 56 * 1024 * 1024


def _sigmoid(x):
    return 0.5 * (jnp.tanh(0.5 * x) + 1.0)


def _silu(x):
    return x * _sigmoid(x)


def _params(sem):
    return pltpu.CompilerParams(dimension_semantics=sem, vmem_limit_bytes=VMEM_LIMIT)


def _mod_kernel(c_ref, w_ref, b_ref, o_ref):
    s = _silu(c_ref[...]).astype(BF16)
    o_ref[...] = jnp.dot(s, w_ref[...].astype(BF16), preferred_element_type=F32) + b_ref[...]


def _modulation(cvec, w_ada, b_ada):
    depth, d, n = w_ada.shape
    tn = 768
    return pl.pallas_call(
        _mod_kernel,
        grid=(depth, n // tn),
        in_specs=[
            pl.BlockSpec((MOD_ROWS, d), lambda l, j: (0, 0)),
            pl.BlockSpec((None, d, tn), lambda l, j: (l, 0, j)),
            pl.BlockSpec((None, 1, tn), lambda l, j: (l, 0, j)),
        ],
        out_specs=pl.BlockSpec((None, MOD_ROWS, tn), lambda l, j: (l, 0, j)),
        out_shape=jax.ShapeDtypeStruct((depth, MOD_ROWS, n), F32),
        compiler_params=_params(("arbitrary", "arbitrary")),
        name="adaln_mod",
    )(cvec, w_ada, b_ada.reshape(depth, 1, n))


NORM_ROWS = 16
NORM_UNROLL = 8


def _prenorm_into(h_ref, x_ref, g_ref, mod_ref):
    gain = g_ref[...] * (1.0 + mod_ref[1])
    shift = mod_ref[0]

    def body(c, carry):
        rows = pl.ds(pl.multiple_of(c * NORM_ROWS, NORM_ROWS), NORM_ROWS)
        x = x_ref[rows, :]
        inv = lax.rsqrt(jnp.mean(x * x, axis=-1, keepdims=True) + EPS)
        h_ref[rows, :] = ((x * inv) * gain + shift).astype(BF16)
        return carry

    lax.fori_loop(0, x_ref.shape[0] // NORM_ROWS, body, 0, unroll=NORM_UNROLL)


def _inproj_kernel(x_ref, mod_ref, g_ref, w_ref, o_ref, h_ref, *, tn):
    _prenorm_into(h_ref, x_ref, g_ref, mod_ref)
    for c in range(o_ref.shape[1] // tn):
        cols = slice(c * tn, (c + 1) * tn)
        o_ref[:, cols] = jnp.dot(h_ref[...], w_ref[:, cols], preferred_element_type=F32)


def _in_projection(x2, mod, g_pre, w_in_b, n_mix, seq, mod_row0, tm, tn):
    rows, d = x2.shape
    if mod_row0 is None:
        mod_map = lambda i: ((i * tm) // seq, 0, 0, 0)
    else:
        mod_map = lambda i: (mod_row0, 0, 0, 0)
    return pl.pallas_call(
        functools.partial(_inproj_kernel, tn=tn),
        grid=(rows // tm,),
        in_specs=[
            pl.BlockSpec((tm, d), lambda i: (i, 0)),
            pl.BlockSpec((None, 3, 1, d), mod_map),
            pl.BlockSpec((1, d), lambda i: (0, 0)),
            pl.BlockSpec((d, n_mix), lambda i: (0, 0), pipeline_mode=pl.Buffered(1)),
        ],
        out_specs=pl.BlockSpec((tm, n_mix), lambda i: (i, 0)),
        out_shape=jax.ShapeDtypeStruct((rows, n_mix), F32),
        scratch_shapes=[pltpu.VMEM((tm, d), BF16)],
        compiler_params=_params(("arbitrary",)),
        name="in_proj",
    )(x2, mod, g_pre, w_in_b)


def _pool_kernel(up_ref, uc_ref, un_ref, z_ref, wm_ref, sc_ref, o_ref, ext_ref, *, seq, tt):
    i = pl.program_id(1)
    nt = pl.num_programs(1)
    halo = SUBLANES
    ext_ref[0:halo, :] = jnp.where(i > 0, up_ref[...], 0.0)
    ext_ref[halo:halo + tt, :] = uc_ref[...]
    ext_ref[halo + tt:2 * halo + tt, :] = jnp.where(i < nt - 1, un_ref[...], 0.0)
    t = i * tt + lax.broadcasted_iota(jnp.int32, (tt, 1), 0)
    for g, w in enumerate(POOL_WINDOWS):
        lanes = slice(g * LANES, (g + 1) * LANES)
        base = halo - w // 2
        acc = ext_ref[base:base + tt, lanes]
        for k in range(1, w):
            acc = acc + ext_ref[base + k:base + k + tt, lanes]
        cnt = (jnp.minimum(t + w // 2, seq) - jnp.maximum(t - w // 2, 0)).astype(F32)
        dlt = acc / cnt - uc_ref[:, lanes]
        y = jnp.dot(dlt.astype(BF16), wm_ref[g], preferred_element_type=F32)
        y = (y * sc_ref[:, lanes]) * _silu(z_ref[:, lanes])
        o_ref[:, lanes] = y.astype(BF16)


def _pool_mixer(mix, w_map_b, pool_scale, tt):
    bsz, seq, _ = mix.shape
    pw = len(POOL_WINDOWS) * LANES
    nh = tt // SUBLANES
    last = seq // SUBLANES - 1
    return pl.pallas_call(
        functools.partial(_pool_kernel, seq=seq, tt=tt),
        grid=(bsz, seq // tt),
        in_specs=[
            pl.BlockSpec((None, SUBLANES, pw), lambda b, i: (b, jnp.maximum(i * nh - 1, 0), 0)),
            pl.BlockSpec((None, tt, pw), lambda b, i: (b, i, 0)),
            pl.BlockSpec((None, SUBLANES, pw), lambda b, i: (b, jnp.minimum((i + 1) * nh, last), 0)),
            pl.BlockSpec((None, tt, pw), lambda b, i: (b, i, 1)),
            pl.BlockSpec((len(POOL_WINDOWS), LANES, LANES), lambda b, i: (0, 0, 0)),
            pl.BlockSpec((1, pw), lambda b, i: (0, 0)),
        ],
        out_specs=pl.BlockSpec((None, tt, pw), lambda b, i: (b, i, 0)),
        out_shape=jax.ShapeDtypeStruct((bsz, seq, pw), BF16),
        scratch_shapes=[pltpu.VMEM((tt + 2 * SUBLANES, pw), F32)],
        compiler_params=_params(("arbitrary", "arbitrary")),
        name="pool_mixer",
    )(mix, mix, mix, mix, w_map_b, pool_scale)


def _stack_heads(x):
    return jnp.concatenate([x[:, g * HEAD_DIM:(g + 1) * HEAD_DIM] for g in range(GQA_G)], axis=0)


def _unstack_heads(x, rows):
    return jnp.concatenate([x[g * rows:(g + 1) * rows, :] for g in range(GQA_G)], axis=1)


def _sink_column(sink_ref, kv, rows):
    head = lax.broadcasted_iota(jnp.int32, (GQA_G * rows, 1), 0) // rows
    col = jnp.full((GQA_G * rows, 1), sink_ref[kv * GQA_G], F32)
    for g in range(1, GQA_G):
        col = jnp.where(head == g, sink_ref[kv * GQA_G + g], col)
    return col


def _softmax_pv(s, valid, sink, v):
    if valid is not None:
        s = jnp.where(valid, s, -jnp.inf)
    m = jnp.maximum(jnp.max(s, axis=-1, keepdims=True), sink)
    e = jnp.exp(s - m)
    den = jnp.sum(e, axis=-1, keepdims=True) + jnp.exp(sink - m)
    o = jnp.dot(e.astype(BF16), v, preferred_element_type=F32)
    return o / den


def _qk(q, k):
    return lax.dot_general(q, k, (((1,), (1,)), ((), ())), preferred_element_type=F32)


def _ctx_attn_kernel(sink_ref, q_ref, k_ref, v_ref, z_ref, o_ref, *, seq):
    kv = pl.program_id(1)
    scale = HEAD_DIM ** -0.5
    q = _stack_heads(q_ref[...]).astype(BF16)
    s = _qk(q, k_ref[...].astype(BF16)) * scale
    o = _softmax_pv(s, None, _sink_column(sink_ref, kv, seq), v_ref[...].astype(BF16))
    o_ref[...] = (_unstack_heads(o, seq) * _silu(z_ref[...])).astype(BF16)


def _context_attention(mix, sink, col):
    bsz, seq, _ = mix.shape
    gw = GQA_G * HEAD_DIM
    return pl.pallas_call(
        functools.partial(_ctx_attn_kernel, seq=seq),
        grid=(bsz, N_KV),
        in_specs=[
            pl.BlockSpec(memory_space=pltpu.SMEM),
            pl.BlockSpec((None, seq, gw), lambda b, kv: (b, 0, col["q"] // gw + kv)),
            pl.BlockSpec((None, seq, HEAD_DIM), lambda b, kv: (b, 0, col["k"] // HEAD_DIM + kv)),
            pl.BlockSpec((None, seq, HEAD_DIM), lambda b, kv: (b, 0, col["v"] // HEAD_DIM + kv)),
            pl.BlockSpec((None, seq, gw), lambda b, kv: (b, 0, col["z_attn"] // gw + kv)),
        ],
        out_specs=pl.BlockSpec((None, seq, gw), lambda b, kv: (b, 0, kv)),
        out_shape=jax.ShapeDtypeStruct((bsz, seq, N_KV * gw), BF16),
        compiler_params=_params(("arbitrary", "arbitrary")),
        name="ctx_attention",
    )(sink, mix, mix, mix, mix)


def _rope(x, cos, sin):
    lane = lax.broadcasted_iota(jnp.int32, x.shape, 1)
    quarter = HEAD_DIM // 4
    first = (lane % (2 * quarter)) < quarter
    partner = jnp.where(first, pltpu.roll(x, HEAD_DIM - quarter, 1), pltpu.roll(x, quarter, 1))
    return x * cos + partner * sin


def _lat_attn_kernel(sink_ref, q_ref, k_ref, v_ref, kx_ref, vx_ref, z_ref, cos_ref, sin_ref, o_ref,
                     kr_ref, vb_ref, bias_ref, *, seq, qc):
    kv = pl.program_id(1)
    ci = pl.program_id(2)
    nb = seq // BLOCK
    nb_c = qc // BLOCK
    span = 3 * BLOCK
    scale = HEAD_DIM ** -0.5

    @pl.when(ci == 0)
    def _():
        zeros = jnp.zeros((BLOCK, HEAD_DIM), BF16)
        for ref in (kr_ref, vb_ref):
            ref[0:BLOCK, :] = zeros
            ref[BLOCK + seq:2 * BLOCK + seq, :] = zeros

        def fill(i, carry):
            rows = pl.ds(pl.multiple_of(i * BLOCK, BLOCK), BLOCK)
            dst = pl.ds(pl.multiple_of((i + 1) * BLOCK, BLOCK), BLOCK)
            kr_ref[dst, :] = _rope(k_ref[rows, :], cos_ref[rows, :], sin_ref[rows, :]).astype(BF16)
            vb_ref[dst, :] = v_ref[rows, :].astype(BF16)
            return carry

        lax.fori_loop(0, nb, fill, 0, unroll=4)
        r = lax.broadcasted_iota(jnp.int32, (GQA_G * BLOCK, span), 0) % BLOCK
        c = lax.broadcasted_iota(jnp.int32, (GQA_G * BLOCK, span), 1)
        band = (c >= r) & (c <= r + 2 * WINDOW)
        neg = jnp.float32(-jnp.inf)
        bias_ref[0] = jnp.where(band & (c >= BLOCK), 0.0, neg)
        bias_ref[1] = jnp.where(band, 0.0, neg)
        bias_ref[2] = jnp.where(band & (c < 2 * BLOCK), 0.0, neg)

    sink = _sink_column(sink_ref, kv, BLOCK)
    kx = kx_ref[...].astype(BF16)
    vx = vx_ref[...].astype(BF16)

    def block(jb, carry):
        bi = ci * nb_c + jb
        rows = pl.ds(pl.multiple_of(jb * BLOCK, BLOCK), BLOCK)
        pos = pl.ds(pl.multiple_of(bi * BLOCK, BLOCK), BLOCK)
        cos_q = cos_ref[pos, :] * scale
        sin_q = sin_ref[pos, :] * scale
        q = jnp.concatenate(
            [_rope(q_ref[rows, g * HEAD_DIM:(g + 1) * HEAD_DIM], cos_q, sin_q) for g in range(GQA_G)],
            axis=0).astype(BF16)
        win = pl.ds(pl.multiple_of(bi * BLOCK, BLOCK), span)
        variant = jnp.where(bi == 0, 0, jnp.where(bi == nb - 1, 2, 1))
        s_w = _qk(q, kr_ref[win, :]) + bias_ref[variant]
        s_c = _qk(q, kx)
        m = jnp.maximum(jnp.maximum(jnp.max(s_w, axis=-1, keepdims=True),
                                    jnp.max(s_c, axis=-1, keepdims=True)), sink)
        e_w = jnp.exp(s_w - m)
        e_c = jnp.exp(s_c - m)
        den = (jnp.sum(e_w, axis=-1, keepdims=True) + jnp.sum(e_c, axis=-1, keepdims=True)
               + jnp.exp(sink - m))
        o = (jnp.dot(e_w.astype(BF16), vb_ref[win, :], preferred_element_type=F32)
             + jnp.dot(e_c.astype(BF16), vx, preferred_element_type=F32)) / den
        o_ref[rows, :] = (_unstack_heads(o, BLOCK) * _silu(z_ref[rows, :])).astype(BF16)
        return carry

    lax.fori_loop(0, nb_c, block, 0, unroll=2)


def _latent_attention(mix, sink, k_ctx, v_ctx, cos_t, sin_t, col):
    bsz, seq, _ = mix.shape
    tc = k_ctx.shape[1]
    assert seq // BLOCK >= 2
    qc = _tile(seq, 1024)
    gw = GQA_G * HEAD_DIM
    kcol = col["k"] // HEAD_DIM
    vcol = col["v"] // HEAD_DIM
    return pl.pallas_call(
        functools.partial(_lat_attn_kernel, seq=seq, qc=qc),
        grid=(bsz, N_KV, seq // qc),
        in_specs=[
            pl.BlockSpec(memory_space=pltpu.SMEM),
            pl.BlockSpec((None, qc, gw), lambda b, kv, ci: (b, ci, col["q"] // gw + kv)),
            pl.BlockSpec((None, seq, HEAD_DIM), lambda b, kv, ci: (b, 0, kcol + kv)),
            pl.BlockSpec((None, seq, HEAD_DIM), lambda b, kv, ci: (b, 0, vcol + kv)),
            pl.BlockSpec((None, tc, HEAD_DIM), lambda b, kv, ci: (b, 0, kv)),
            pl.BlockSpec((None, tc, HEAD_DIM), lambda b, kv, ci: (b, 0, kv)),
            pl.BlockSpec((None, qc, gw), lambda b, kv, ci: (b, ci, col["z_attn"] // gw + kv)),
            pl.BlockSpec((seq, HEAD_DIM), lambda b, kv, ci: (0, 0)),
            pl.BlockSpec((seq, HEAD_DIM), lambda b, kv, ci: (0, 0)),
        ],
        out_specs=pl.BlockSpec((None, qc, gw), lambda b, kv, ci: (b, ci, kv)),
        out_shape=jax.ShapeDtypeStruct((bsz, seq, N_KV * gw), BF16),
        scratch_shapes=[
            pltpu.VMEM((seq + 2 * BLOCK, HEAD_DIM), BF16),
            pltpu.VMEM((seq + 2 * BLOCK, HEAD_DIM), BF16),
            pltpu.VMEM((3, GQA_G * BLOCK, 3 * BLOCK), F32),
        ],
        compiler_params=_params(("arbitrary", "arbitrary", "arbitrary")),
        name="latent_attention",
    )(sink, mix, mix, mix, k_ctx, v_ctx, mix, cos_t, sin_t)


def _rope_tables(seq):
    rows = seq // GRID_W
    row = jnp.repeat(jnp.arange(rows, dtype=F32), GRID_W)
    colp = jnp.tile(jnp.arange(GRID_W, dtype=F32), rows)
    n_freq = HEAD_DIM // 4
    inv = ROPE_BASE ** (-jnp.arange(n_freq, dtype=F32) / n_freq)
    ang_r = row[:, None] * inv[None]
    ang_c = colp[:, None] * inv[None]
    cos_t = jnp.concatenate([jnp.cos(ang_r)] * 2 + [jnp.cos(ang_c)] * 2, axis=-1)
    sin_t = jnp.concatenate([-jnp.sin(ang_r), jnp.sin(ang_r), -jnp.sin(ang_c), jnp.sin(ang_c)], axis=-1)
    return cos_t, sin_t


def _row_bcast(x, r):
    return jnp.broadcast_to(x[r:r + 1, :], x.shape)


SCAN_ROWS = SUBLANES * SUBLANES


def _segment_scan(a_ref, b_ref, t0, carry, reverse):
    idx = [pl.ds(t0 + k, SUBLANES, stride=SUBLANES) for k in range(SUBLANES)]
    a = [a_ref[i, :] for i in idx]
    b = [b_ref[i, :] for i in idx]
    order = list(range(SUBLANES - 1, -1, -1)) if reverse else list(range(SUBLANES))
    h = [None] * SUBLANES
    p = [None] * SUBLANES
    prev = None
    for k in order:
        h[k] = b[k] if prev is None else a[k] * h[prev] + b[k]
        p[k] = a[k] if prev is None else a[k] * p[prev]
        prev = k
    pp, hh = p[prev], h[prev]
    sub = lax.broadcasted_iota(jnp.int32, (SUBLANES, LANES), 0)
    for d in (1, 2, 4):
        keep = (sub < SUBLANES - d) if reverse else (sub >= d)
        shift = SUBLANES - d if reverse else d
        p_sh = jnp.where(keep, pltpu.roll(pp, shift, 0), 1.0)
        h_sh = jnp.where(keep, pltpu.roll(hh, shift, 0), 0.0)
        hh = pp * h_sh + hh
        pp = pp * p_sh
    end = pp * carry + hh
    if reverse:
        c_in = jnp.where(sub < SUBLANES - 1, pltpu.roll(end, SUBLANES - 1, 0), carry)
    else:
        c_in = jnp.where(sub >= 1, pltpu.roll(end, 1, 0), carry)
    for k in range(SUBLANES):
        b_ref[idx[k], :] = h[k] + p[k] * c_in
    return _row_bcast(end, 0 if reverse else SUBLANES - 1)


def _lru_kernel(u_ref, z_ref, cw_ref, cb_ref, gw_ref, gb_ref, lam_ref, h0_ref, y_ref, hfin_ref,
                up_ref, af_ref, bf_ref, ab_ref, bb_ref, *, seq, chunk):
    pad = SUBLANES
    up_ref[0:pad, :] = jnp.zeros((pad, LANES), F32)
    up_ref[pad:pad + seq, :] = u_ref[...]
    up_ref[pad + seq:2 * pad + seq, :] = jnp.zeros((pad, LANES), F32)
    nlam = -lam_ref[...]
    softplus = jnp.maximum(nlam, 0.0) + jnp.log1p(jnp.exp(-jnp.abs(nlam)))
    half_rate = (-0.5 * LRU_C) * softplus

    def gates(ci, carry):
        rows = pl.ds(pl.multiple_of(ci * chunk, chunk), chunk)
        xc = cb_ref[...]
        for j in range(CONV_W):
            tap = up_ref[pl.ds(ci * chunk + (pad - CONV_LEFT + j), chunk), :]
            xc = xc + tap * cw_ref[j:j + 1, :]
        g = jnp.dot(xc.astype(BF16), gw_ref[...], preferred_element_type=F32) + gb_ref[...]
        xc_half = 0.5 * xc
        for d, (a_ref, b_ref) in enumerate(((af_ref, bf_ref), (ab_ref, bb_ref))):
            t_r = jnp.tanh(g[:, (2 * d) * LANES:(2 * d + 1) * LANES])
            t_i = jnp.tanh(g[:, (2 * d + 1) * LANES:(2 * d + 2) * LANES])
            log_a = (t_r + 1.0) * half_rate[d:d + 1, :]
            a = jnp.exp(log_a)
            one_minus_a2 = -jnp.tanh(log_a) * (a * a + 1.0)
            a_ref[rows, :] = a
            b_ref[rows, :] = jnp.sqrt(jnp.maximum(one_minus_a2, 0.0)) * ((t_i + 1.0) * xc_half)
        return carry

    lax.fori_loop(0, seq // chunk, gates, 0, unroll=min(4, seq // chunk))

    n_scan = seq // SCAN_ROWS

    def scan(ci, carry):
        c_f, c_b = carry
        c_f = _segment_scan(af_ref, bf_ref, ci * SCAN_ROWS, c_f, False)
        c_b = _segment_scan(ab_ref, bb_ref, (n_scan - 1 - ci) * SCAN_ROWS, c_b, True)
        return c_f, c_b

    h0 = h0_ref[...]
    c_f, c_b = lax.fori_loop(0, n_scan, scan, (jnp.broadcast_to(h0[0:1, :], (SUBLANES, LANES)),
                                               jnp.broadcast_to(h0[1:2, :], (SUBLANES, LANES))),
                             unroll=2)
    hfin_ref[0:1, :] = c_f[0:1, :]
    hfin_ref[1:2, :] = c_b[0:1, :]

    def gate_out(ci, carry):
        rows = pl.ds(pl.multiple_of(ci * chunk, chunk), chunk)
        y_ref[rows, :] = ((bf_ref[rows, :] + bb_ref[rows, :]) * _silu(z_ref[rows, :])).astype(BF16)
        return carry

    lax.fori_loop(0, seq // chunk, gate_out, 0, unroll=2)


def _rglru(mix, conv_w, conv_b, gate_w_b, gate_b, lam, h0, col, chunk):
    bsz, seq, _ = mix.shape
    assert seq % SCAN_ROWS == 0 and seq % chunk == 0
    lw = LRU_BLOCKS * LANES
    ucol = col["u_lru"] // LANES
    zcol = col["z_lru"] // LANES
    return pl.pallas_call(
        functools.partial(_lru_kernel, seq=seq, chunk=chunk),
        grid=(bsz, LRU_BLOCKS),
        in_specs=[
            pl.BlockSpec((None, seq, LANES), lambda b, n: (b, 0, ucol + n)),
            pl.BlockSpec((None, seq, LANES), lambda b, n: (b, 0, zcol + n)),
            pl.BlockSpec((CONV_W, LANES), lambda b, n: (0, n)),
            pl.BlockSpec((1, LANES), lambda b, n: (0, n)),
            pl.BlockSpec((None, LANES, 4 * LANES), lambda b, n: (n, 0, 0)),
            pl.BlockSpec((None, 1, 4 * LANES), lambda b, n: (n, 0, 0)),
            pl.BlockSpec((2, LANES), lambda b, n: (0, n)),
            pl.BlockSpec((None, 2, LANES), lambda b, n: (b, 0, n)),
        ],
        out_specs=[
            pl.BlockSpec((None, seq, LANES), lambda b, n: (b, 0, n)),
            pl.BlockSpec((None, 2, LANES), lambda b, n: (b, 0, n)),
        ],
        out_shape=[
            jax.ShapeDtypeStruct((bsz, seq, lw), BF16),
            jax.ShapeDtypeStruct((bsz, 2, lw), F32),
        ],
        scratch_shapes=[pltpu.VMEM((seq + 2 * SUBLANES, LANES), F32)]
        + [pltpu.VMEM((seq, LANES), F32)] * 4,
        compiler_params=_params(("arbitrary", "arbitrary")),
        name="rglru",
    )(mix, mix, conv_w, conv_b, gate_w_b, gate_b, lam, h0)


def _out_kernel(x_ref, mod_ref, gpre_ref, gpost_ref, yp_ref, ya_ref, yl_ref,
                wm0_ref, wm1_ref, wm2_ref, wp_ref, wa_ref, wl_ref, wo_ref, o_ref, h_ref, mg_ref, acc_ref,
                *, n_merge, n_out, tn, tn_out):
    j = pl.program_id(1)

    @pl.when(j == 0)
    def _():
        _prenorm_into(h_ref, x_ref, gpre_ref, mod_ref)

    @pl.when(j < n_merge)
    def _():
        h = h_ref[...]
        merged = None
        for wm_ref, y_ref, wb_ref in ((wm0_ref, yp_ref, wp_ref), (wm1_ref, ya_ref, wa_ref),
                                      (wm2_ref, yl_ref, wl_ref)):
            gate = _sigmoid(jnp.dot(h, wm_ref[...], preferred_element_type=F32))
            term = gate * jnp.dot(y_ref[...], wb_ref[...], preferred_element_type=F32)
            merged = term if merged is None else merged + term
        merged = merged.astype(BF16)
        for c in range(n_merge):
            @pl.when(j == c)
            def _():
                mg_ref[:, c * tn:(c + 1) * tn] = merged

    @pl.when(j == n_merge)
    def _():
        for c in range(n_out):
            cols = slice(c * tn_out, (c + 1) * tn_out)
            acc_ref[:, cols] = jnp.dot(mg_ref[...], wo_ref[:, cols], preferred_element_type=F32)
        gain = gpost_ref[...] * mod_ref[2]

        def body(c, carry):
            rows = pl.ds(pl.multiple_of(c * NORM_ROWS, NORM_ROWS), NORM_ROWS)
            out = acc_ref[rows, :]
            inv = lax.rsqrt(jnp.mean(out * out, axis=-1, keepdims=True) + EPS)
            o_ref[rows, :] = x_ref[rows, :] + (out * inv) * gain
            return carry

        lax.fori_loop(0, o_ref.shape[0] // NORM_ROWS, body, 0, unroll=NORM_UNROLL)


def _out_stage(x2, mod, g_pre, g_post, y_pool, y_attn, y_lru, w_in_b, w_pool_o, w_attn_o, w_lru_o,
               w_out, n_mix, seq, mod_row0, tm, tn, tn_out):
    rows, d = x2.shape
    n_merge = d // tn
    n_out = d // tn_out
    if mod_row0 is None:
        mod_map = lambda i, j: ((i * tm) // seq, 0, 0, 0)
    else:
        mod_map = lambda i, j: (mod_row0, 0, 0, 0)
    mstep = lambda j: jnp.minimum(j, n_merge - 1)
    mcol = lambda br: (lambda i, j: (0, (n_mix + br * d) // tn + mstep(j)))
    bcol = lambda i, j: (0, mstep(j))
    return pl.pallas_call(
        functools.partial(_out_kernel, n_merge=n_merge, n_out=n_out, tn=tn, tn_out=tn_out),
        grid=(rows // tm, n_merge + 1),
        in_specs=[
            pl.BlockSpec((tm, d), lambda i, j: (i, 0)),
            pl.BlockSpec((None, 3, 1, d), mod_map),
            pl.BlockSpec((1, d), lambda i, j: (0, 0)),
            pl.BlockSpec((1, d), lambda i, j: (0, 0)),
            pl.BlockSpec((tm, y_pool.shape[1]), lambda i, j: (i, 0)),
            pl.BlockSpec((tm, y_attn.shape[1]), lambda i, j: (i, 0)),
            pl.BlockSpec((tm, y_lru.shape[1]), lambda i, j: (i, 0)),
            pl.BlockSpec((d, tn), mcol(0)),
            pl.BlockSpec((d, tn), mcol(1)),
            pl.BlockSpec((d, tn), mcol(2)),
            pl.BlockSpec((w_pool_o.shape[0], tn), bcol),
            pl.BlockSpec((w_attn_o.shape[0], tn), bcol),
            pl.BlockSpec((w_lru_o.shape[0], tn), bcol),
            pl.BlockSpec((d, d), lambda i, j: (0, 0), pipeline_mode=pl.Buffered(1)),
        ],
        out_specs=pl.BlockSpec((tm, d), lambda i, j: (i, 0)),
        out_shape=jax.ShapeDtypeStruct((rows, d), F32),
        scratch_shapes=[pltpu.VMEM((tm, d), BF16), pltpu.VMEM((tm, d), BF16),
                        pltpu.VMEM((tm, d), F32)],
        compiler_params=_params(("arbitrary", "arbitrary")),
        name="out_stage",
    )(x2, mod, g_pre, g_post, y_pool, y_attn, y_lru, w_in_b, w_in_b, w_in_b,
      w_pool_o, w_attn_o, w_lru_o, w_out)


def _tile(n, want):
    t = min(n, want)
    assert n % t == 0, (n, t)
    return t


def _trunk_layer(x, mod, mod_row0, P, col, n_mix, ctx):
    bsz, seq, d = x.shape
    x2 = x.reshape(bsz * seq, d)
    rows = bsz * seq
    tm_in = _tile(rows if mod_row0 is not None else seq, 512)
    mix = _in_projection(x2, mod, P["g_pre"], P["w_in"], n_mix, seq, mod_row0, tm_in, 768)
    mix = mix.reshape(bsz, seq, n_mix)
    y_pool = _pool_mixer(mix, P["w_pool_map"], P["pool_scale"], _tile(seq, 512))
    if ctx is None:
        y_attn = _context_attention(mix, P["attn_sink"], col)
        h0 = jnp.zeros((bsz, 2, LRU_BLOCKS * LANES), F32)
    else:
        k_ctx, v_ctx, h0, cos_t, sin_t = ctx
        y_attn = _latent_attention(mix, P["attn_sink"], k_ctx, v_ctx, cos_t, sin_t, col)
    y_lru, h_fin = _rglru(mix, P["lru_conv_w"], P["lru_conv_b"], P["lru_gate_w"], P["lru_gate_b"],
                          P["lru_lambda"], h0, col, _tile(seq, 128))
    tm_out = _tile(rows if mod_row0 is not None else seq, 512)
    x_new = _out_stage(x2, mod, P["g_pre"], P["g_post"], y_pool.reshape(rows, -1),
                       y_attn.reshape(rows, -1), y_lru.reshape(rows, -1), P["w_in"], P["w_pool_o"],
                       P["w_attn_o"], P["w_lru_o"], P["w_out"], n_mix, seq, mod_row0, tm_out, 256, 512)
    return x_new.reshape(bsz, seq, d), mix, h_fin


def kernel(x_prompt, x_sample, cache_k, cache_v, state_lru, c, c_ctx, g_pre, g_post, w_ada, b_ada,
           w_in, w_pool_map, pool_scale, attn_sink, lru_conv_w, lru_conv_b, lru_gate_w, lru_gate_b,
           lru_lambda, w_pool_o, w_attn_o, w_lru_o, w_out):
    depth = w_in.shape[0]
    d = x_prompt.shape[-1]
    dec_b, dec_seq, _ = x_sample.shape
    pool_w = len(POOL_WINDOWS) * LANES
    attn_w = N_HEADS * HEAD_DIM
    kv_w = N_KV * HEAD_DIM
    lru_w = LRU_BLOCKS * LANES
    widths = (("u_pool", pool_w), ("z_pool", pool_w), ("q", attn_w), ("k", kv_w), ("v", kv_w),
              ("z_attn", attn_w), ("u_lru", lru_w), ("z_lru", lru_w))
    col, n_mix = {}, 0
    for name, w in widths:
        col[name] = n_mix
        n_mix += w
    assert w_in.shape[-1] == n_mix + N_BRANCH * d
    assert dec_b < MOD_ROWS

    cvec = jnp.zeros((MOD_ROWS, d), F32).at[:dec_b].set(c).at[dec_b].set(c_ctx)
    mod = _modulation(cvec, w_ada, b_ada)
    mod = mod.reshape(depth, MOD_ROWS, 3, 1, d)
    cos_t, sin_t = _rope_tables(dec_seq)
    past = cache_k.shape[2]
    cache_k2 = cache_k.reshape(dec_b, depth, past, kv_w)
    cache_v2 = cache_v.reshape(dec_b, depth, past, kv_w)

    xp, xs = x_prompt, x_sample
    new_k, new_v, new_h = [], [], []
    for l in range(depth):
        P = {
            "g_pre": g_pre[l][None], "g_post": g_post[l][None],
            "w_in": w_in[l].astype(BF16),
            "w_pool_map": w_pool_map[l].astype(BF16), "pool_scale": pool_scale[l][None],
            "attn_sink": attn_sink[l],
            "lru_conv_w": lru_conv_w[l], "lru_conv_b": lru_conv_b[l][None],
            "lru_gate_w": (0.5 * lru_gate_w[l]).transpose(2, 3, 0, 1, 4).reshape(
                LRU_BLOCKS, LANES, 4 * LANES).astype(BF16),
            "lru_gate_b": (0.5 * lru_gate_b[l]).reshape(2, 2, LRU_BLOCKS, LANES).transpose(
                2, 0, 1, 3).reshape(LRU_BLOCKS, 1, 4 * LANES),
            "lru_lambda": lru_lambda[l],
            "w_pool_o": w_pool_o[l].astype(BF16), "w_attn_o": w_attn_o[l].astype(BF16),
            "w_lru_o": w_lru_o[l].astype(BF16), "w_out": w_out[l].astype(BF16),
        }
        xp, mix_p, h_fin = _trunk_layer(xp, mod[l], dec_b, P, col, n_mix, None)
        bsz, seq = xp.shape[:2]
        new_k.append(mix_p[:, :, col["k"]:col["k"] + kv_w].reshape(bsz, seq, N_KV, HEAD_DIM))
        new_v.append(mix_p[:, :, col["v"]:col["v"] + kv_w].reshape(bsz, seq, N_KV, HEAD_DIM))
        new_h.append(h_fin)
        ctx = (cache_k2[:, l], cache_v2[:, l], state_lru[:, l], cos_t, sin_t)
        xs, _, _ = _trunk_layer(xs, mod[l], None, P, col, n_mix, ctx)
    return (xp, xs, jnp.stack(new_k, axis=1), jnp.stack(new_v, axis=1), jnp.stack(new_h, axis=1))
```

```python
import functools

import jax
import jax.numpy as jnp
from jax import lax
from jax.experimental import pallas as pl
from jax.experimental.pallas import tpu as pltpu

F32 = jnp.float32
BF16 = jnp.bfloat16

EPS = 1e-6
LANES = 128
SUBLANES = 8
POOL_WINDOWS = (2, 4, 8, 16)
N_HEADS = 8
N_KV = 2
HEAD_DIM = 128
GQA_G = N_HEADS // N_KV
WINDOW = 128
BLOCK = 128
GRID_W = 64
ROPE_BASE = 10000.0
LRU_BLOCKS = 4
CONV_W = 4
CONV_LEFT = 2
LRU_C = 8.0
N_BRANCH = 3
MOD_ROWS = 16

VMEM_LIMIT = 60 * 1024 * 1024


def _sigmoid(x):
    return 0.5 * (jnp.tanh(0.5 * x) + 1.0)


def _silu(x):
    return x * _sigmoid(x)


def _params(sem):
    return pltpu.CompilerParams(dimension_semantics=sem, vmem_limit_bytes=VMEM_LIMIT)


def _mod_kernel(c_ref, w_ref, b_ref, o_ref):
    s = _silu(c_ref[...]).astype(BF16)
    o_ref[...] = jnp.dot(s, w_ref[...].astype(BF16), preferred_element_type=F32) + b_ref[...]


def _modulation(cvec, w_ada, b_ada):
    depth, d, n = w_ada.shape
    tn = 768
    return pl.pallas_call(
        _mod_kernel,
        grid=(depth, n // tn),
        in_specs=[
            pl.BlockSpec((MOD_ROWS, d), lambda l, j: (0, 0)),
            pl.BlockSpec((None, d, tn), lambda l, j: (l, 0, j)),
            pl.BlockSpec((None, 1, tn), lambda l, j: (l, 0, j)),
        ],
        out_specs=pl.BlockSpec((None, MOD_ROWS, tn), lambda l, j: (l, 0, j)),
        out_shape=jax.ShapeDtypeStruct((depth, MOD_ROWS, n), F32),
        compiler_params=_params(("arbitrary", "arbitrary")),
        name="adaln_mod",
    )(cvec, w_ada, b_ada.reshape(depth, 1, n))


NORM_ROWS = 16
NORM_UNROLL = 8


def _prenorm_into(h_ref, x_ref, g_ref, mod_ref):
    gain = g_ref[...] * (1.0 + mod_ref[1])
    shift = mod_ref[0]

    def body(c, carry):
        rows = pl.ds(pl.multiple_of(c * NORM_ROWS, NORM_ROWS), NORM_ROWS)
        x = x_ref[rows, :]
        inv = lax.rsqrt(jnp.mean(x * x, axis=-1, keepdims=True) + EPS)
        h_ref[rows, :] = ((x * inv) * gain + shift).astype(BF16)
        return carry

    lax.fori_loop(0, x_ref.shape[0] // NORM_ROWS, body, 0, unroll=NORM_UNROLL)


def _inproj_kernel(x_ref, mod_ref, g_ref, w_ref, o_ref, h_ref, *, tn):
    _prenorm_into(h_ref, x_ref, g_ref, mod_ref)
    for c in range(o_ref.shape[1] // tn):
        cols = slice(c * tn, (c + 1) * tn)
        o_ref[:, cols] = jnp.dot(h_ref[...], w_ref[:, cols], preferred_element_type=F32)


def _in_projection(x2, mod, g_pre, w_in_b, n_mix, seq, mod_row0, tm, tn):
    rows, d = x2.shape
    if mod_row0 is None:
        mod_map = lambda i: ((i * tm) // seq, 0, 0, 0)
    else:
        mod_map = lambda i: (mod_row0, 0, 0, 0)
    return pl.pallas_call(
        functools.partial(_inproj_kernel, tn=tn),
        grid=(rows // tm,),
        in_specs=[
            pl.BlockSpec((tm, d), lambda i: (i, 0)),
            pl.BlockSpec((None, 3, 1, d), mod_map),
            pl.BlockSpec((1, d), lambda i: (0, 0)),
            pl.BlockSpec((d, n_mix), lambda i: (0, 0), pipeline_mode=pl.Buffered(1)),
        ],
        out_specs=[pl.BlockSpec((tm, n_mix), lambda i: (i, 0)),
                   pl.BlockSpec((tm, d), lambda i: (i, 0))],
        out_shape=[jax.ShapeDtypeStruct((rows, n_mix), F32),
                   jax.ShapeDtypeStruct((rows, d), BF16)],
        compiler_params=_params(("arbitrary",)),
        name="in_proj",
    )(x2, mod, g_pre, w_in_b)


POOL_EDGE = 16


def _window_sum(ext_ref, pa_ref, pb_ref, lanes, base, w, tt):
    if w <= 4:
        acc = ext_ref[base:base + tt, lanes]
        for k in range(1, w):
            acc = acc + ext_ref[base + k:base + k + tt, lanes]
        return acc
    n = tt + w - 2
    pa_ref[0:n, :] = ext_ref[base:base + n, lanes] + ext_ref[base + 1:base + 1 + n, lanes]
    src, dst, s = pa_ref, pb_ref, 2
    while 2 * s < w:
        n = tt + w - 2 * s
        dst[0:n, :] = src[0:n, :] + src[s:s + n, :]
        src, dst, s = dst, src, 2 * s
    return src[0:tt, :] + src[s:s + tt, :]


def _pool_kernel(up_ref, uc_ref, un_ref, z_ref, wm_ref, sc_ref, o_ref, ext_ref, pa_ref, pb_ref,
                 d_ref, *, seq, tt):
    i = pl.program_id(1)
    nt = pl.num_programs(1)
    halo = SUBLANES
    ext_ref[0:halo, :] = jnp.where(i > 0, up_ref[...], 0.0)
    ext_ref[halo:halo + tt, :] = uc_ref[...]
    ext_ref[halo + tt:2 * halo + tt, :] = jnp.where(i < nt - 1, un_ref[...], 0.0)
    for g, w in enumerate(POOL_WINDOWS):
        lanes = slice(g * LANES, (g + 1) * LANES)
        acc = _window_sum(ext_ref, pa_ref, pb_ref, lanes, halo - w // 2, w, tt)
        d_ref[:, lanes] = (acc * (1.0 / w) - uc_ref[:, lanes]).astype(BF16)
        for r0 in (0, tt - POOL_EDGE):
            t = i * tt + r0 + lax.broadcasted_iota(jnp.int32, (POOL_EDGE, 1), 0)
            cnt = (jnp.minimum(t + w // 2, seq) - jnp.maximum(t - w // 2, 0)).astype(F32)
            edge = acc[r0:r0 + POOL_EDGE, :] / cnt - uc_ref[r0:r0 + POOL_EDGE, lanes]
            d_ref[r0:r0 + POOL_EDGE, lanes] = edge.astype(BF16)
        y = jnp.dot(d_ref[:, lanes], wm_ref[g], preferred_element_type=F32)
        y = (y * sc_ref[:, lanes]) * _silu(z_ref[:, lanes])
        o_ref[:, lanes] = y.astype(BF16)


def _pool_mixer(mix, w_map_b, pool_scale, tt):
    bsz, seq, _ = mix.shape
    pw = len(POOL_WINDOWS) * LANES
    nh = tt // SUBLANES
    last = seq // SUBLANES - 1
    return pl.pallas_call(
        functools.partial(_pool_kernel, seq=seq, tt=tt),
        grid=(bsz, seq // tt),
        in_specs=[
            pl.BlockSpec((None, SUBLANES, pw), lambda b, i: (b, jnp.maximum(i * nh - 1, 0), 0)),
            pl.BlockSpec((None, tt, pw), lambda b, i: (b, i, 0)),
            pl.BlockSpec((None, SUBLANES, pw), lambda b, i: (b, jnp.minimum((i + 1) * nh, last), 0)),
            pl.BlockSpec((None, tt, pw), lambda b, i: (b, i, 1)),
            pl.BlockSpec((len(POOL_WINDOWS), LANES, LANES), lambda b, i: (0, 0, 0)),
            pl.BlockSpec((1, pw), lambda b, i: (0, 0)),
        ],
        out_specs=pl.BlockSpec((None, tt, pw), lambda b, i: (b, i, 0)),
        out_shape=jax.ShapeDtypeStruct((bsz, seq, pw), BF16),
        scratch_shapes=[pltpu.VMEM((tt + 2 * SUBLANES, pw), F32),
                        pltpu.VMEM((tt + 2 * SUBLANES, LANES), F32),
                        pltpu.VMEM((tt + 2 * SUBLANES, LANES), F32),
                        pltpu.VMEM((tt, pw), BF16)],
        compiler_params=_params(("arbitrary", "arbitrary")),
        name="pool_mixer",
    )(mix, mix, mix, mix, w_map_b, pool_scale)


def _stack_heads(x):
    return jnp.concatenate([x[:, g * HEAD_DIM:(g + 1) * HEAD_DIM] for g in range(GQA_G)], axis=0)


def _unstack_heads(x, rows):
    return jnp.concatenate([x[g * rows:(g + 1) * rows, :] for g in range(GQA_G)], axis=1)


def _sink_column(sink_ref, kv, rows):
    head = lax.broadcasted_iota(jnp.int32, (GQA_G * rows, 1), 0) // rows
    col = jnp.full((GQA_G * rows, 1), sink_ref[kv * GQA_G], F32)
    for g in range(1, GQA_G):
        col = jnp.where(head == g, sink_ref[kv * GQA_G + g], col)
    return col


LOG2E = 1.4426950408889634


def _softmax_pv(s, sink, v):
    m = jnp.maximum(jnp.max(s, axis=-1, keepdims=True), sink)
    e = jnp.exp2(s - m)
    den = jnp.sum(e, axis=-1, keepdims=True) + jnp.exp2(sink - m)
    o = jnp.dot(e.astype(BF16), v, preferred_element_type=F32)
    return o / den


def _qk(q, k):
    return lax.dot_general(q, k, (((1,), (1,)), ((), ())), preferred_element_type=F32)


def _ctx_attn_kernel(sink_ref, q_ref, k_ref, v_ref, z_ref, o_ref, *, seq):
    kv = pl.program_id(1)
    scale = HEAD_DIM ** -0.5 * LOG2E
    q = _stack_heads(q_ref[...]).astype(BF16)
    s = _qk(q, k_ref[...].astype(BF16)) * scale
    o = _softmax_pv(s, _sink_column(sink_ref, kv, seq) * LOG2E, v_ref[...].astype(BF16))
    o_ref[...] = (_unstack_heads(o, seq) * _silu(z_ref[...])).astype(BF16)


def _context_attention(mix, sink, col):
    bsz, seq, _ = mix.shape
    gw = GQA_G * HEAD_DIM
    return pl.pallas_call(
        functools.partial(_ctx_attn_kernel, seq=seq),
        grid=(bsz, N_KV),
        in_specs=[
            pl.BlockSpec(memory_space=pltpu.SMEM),
            pl.BlockSpec((None, seq, gw), lambda b, kv: (b, 0, col["q"] // gw + kv)),
            pl.BlockSpec((None, seq, HEAD_DIM), lambda b, kv: (b, 0, col["k"] // HEAD_DIM + kv)),
            pl.BlockSpec((None, seq, HEAD_DIM), lambda b, kv: (b, 0, col["v"] // HEAD_DIM + kv)),
            pl.BlockSpec((None, seq, gw), lambda b, kv: (b, 0, col["z_attn"] // gw + kv)),
        ],
        out_specs=pl.BlockSpec((None, seq, gw), lambda b, kv: (b, 0, kv)),
        out_shape=jax.ShapeDtypeStruct((bsz, seq, N_KV * gw), BF16),
        compiler_params=_params(("arbitrary", "arbitrary")),
        name="ctx_attention",
    )(sink, mix, mix, mix, mix)


def _rope(x, cos, sin):
    lane = lax.broadcasted_iota(jnp.int32, x.shape, 1)
    quarter = HEAD_DIM // 4
    first = (lane % (2 * quarter)) < quarter
    partner = jnp.where(first, pltpu.roll(x, HEAD_DIM - quarter, 1), pltpu.roll(x, quarter, 1))
    return x * cos + partner * sin


def _lat_attn_kernel(sink_ref, q_ref, k_ref, v_ref, kx_ref, vx_ref, z_ref, cos_ref, sin_ref, o_ref,
                     kr_ref, vb_ref, bias_ref, *, seq, qc):
    kv = pl.program_id(1)
    ci = pl.program_id(2)
    nb = seq // BLOCK
    nb_c = qc // BLOCK
    span = 3 * BLOCK
    scale = HEAD_DIM ** -0.5 * LOG2E

    @pl.when(ci == 0)
    def _():
        zeros = jnp.zeros((BLOCK, HEAD_DIM), BF16)
        for ref in (kr_ref, vb_ref):
            ref[0:BLOCK, :] = zeros
            ref[BLOCK + seq:2 * BLOCK + seq, :] = zeros

        def fill(i, carry):
            rows = pl.ds(pl.multiple_of(i * BLOCK, BLOCK), BLOCK)
            dst = pl.ds(pl.multiple_of((i + 1) * BLOCK, BLOCK), BLOCK)
            kr_ref[dst, :] = _rope(k_ref[rows, :], cos_ref[rows, :], sin_ref[rows, :]).astype(BF16)
            vb_ref[dst, :] = v_ref[rows, :].astype(BF16)
            return carry

        lax.fori_loop(0, nb, fill, 0, unroll=4)
        r = lax.broadcasted_iota(jnp.int32, (GQA_G * BLOCK, span), 0) % BLOCK
        c = lax.broadcasted_iota(jnp.int32, (GQA_G * BLOCK, span), 1)
        band = (c >= r) & (c <= r + 2 * WINDOW)
        neg = jnp.float32(-jnp.inf)
        bias_ref[0] = jnp.where(band & (c >= BLOCK), 0.0, neg)
        bias_ref[1] = jnp.where(band, 0.0, neg)
        bias_ref[2] = jnp.where(band & (c < 2 * BLOCK), 0.0, neg)

    sink = _sink_column(sink_ref, kv, BLOCK) * LOG2E
    kx = kx_ref[...].astype(BF16)
    vx = vx_ref[...].astype(BF16)

    def block(jb, carry):
        bi = ci * nb_c + jb
        rows = pl.ds(pl.multiple_of(jb * BLOCK, BLOCK), BLOCK)
        pos = pl.ds(pl.multiple_of(bi * BLOCK, BLOCK), BLOCK)
        cos_q = cos_ref[pos, :] * scale
        sin_q = sin_ref[pos, :] * scale
        q = jnp.concatenate(
            [_rope(q_ref[rows, g * HEAD_DIM:(g + 1) * HEAD_DIM], cos_q, sin_q) for g in range(GQA_G)],
            axis=0).astype(BF16)
        win = pl.ds(pl.multiple_of(bi * BLOCK, BLOCK), span)
        variant = jnp.where(bi == 0, 0, jnp.where(bi == nb - 1, 2, 1))
        s = jnp.concatenate([_qk(q, kr_ref[win, :]) + bias_ref[variant], _qk(q, kx)], axis=1)
        m = jnp.maximum(jnp.max(s, axis=-1, keepdims=True), sink)
        e = jnp.exp2(s - m)
        den = jnp.sum(e, axis=-1, keepdims=True) + jnp.exp2(sink - m)
        e = e.astype(BF16)
        o = (jnp.dot(e[:, :span], vb_ref[win, :], preferred_element_type=F32)
             + jnp.dot(e[:, span:], vx, preferred_element_type=F32)) / den
        o_ref[rows, :] = (_unstack_heads(o, BLOCK) * _silu(z_ref[rows, :])).astype(BF16)
        return carry

    lax.fori_loop(0, nb_c, block, 0, unroll=min(4, nb_c))


def _latent_attention(mix, sink, k_ctx, v_ctx, cos_t, sin_t, col):
    bsz, seq, _ = mix.shape
    tc = k_ctx.shape[1]
    assert seq // BLOCK >= 2
    qc = _tile(seq, 1024)
    gw = GQA_G * HEAD_DIM
    kcol = col["k"] // HEAD_DIM
    vcol = col["v"] // HEAD_DIM
    return pl.pallas_call(
        functools.partial(_lat_attn_kernel, seq=seq, qc=qc),
        grid=(bsz, N_KV, seq // qc),
        in_specs=[
            pl.BlockSpec(memory_space=pltpu.SMEM),
            pl.BlockSpec((None, qc, gw), lambda b, kv, ci: (b, ci, col["q"] // gw + kv)),
            pl.BlockSpec((None, seq, HEAD_DIM), lambda b, kv, ci: (b, 0, kcol + kv)),
            pl.BlockSpec((None, seq, HEAD_DIM), lambda b, kv, ci: (b, 0, vcol + kv)),
            pl.BlockSpec((None, tc, HEAD_DIM), lambda b, kv, ci: (b, 0, kv)),
            pl.BlockSpec((None, tc, HEAD_DIM), lambda b, kv, ci: (b, 0, kv)),
            pl.BlockSpec((None, qc, gw), lambda b, kv, ci: (b, ci, col["z_attn"] // gw + kv)),
            pl.BlockSpec((seq, HEAD_DIM), lambda b, kv, ci: (0, 0)),
            pl.BlockSpec((seq, HEAD_DIM), lambda b, kv, ci: (0, 0)),
        ],
        out_specs=pl.BlockSpec((None, qc, gw), lambda b, kv, ci: (b, ci, kv)),
        out_shape=jax.ShapeDtypeStruct((bsz, seq, N_KV * gw), BF16),
        scratch_shapes=[
            pltpu.VMEM((seq + 2 * BLOCK, HEAD_DIM), BF16),
            pltpu.VMEM((seq + 2 * BLOCK, HEAD_DIM), BF16),
            pltpu.VMEM((3, GQA_G * BLOCK, 3 * BLOCK), F32),
        ],
        compiler_params=_params(("arbitrary", "arbitrary", "arbitrary")),
        name="latent_attention",
    )(sink, mix, mix, mix, k_ctx, v_ctx, mix, cos_t, sin_t)


def _rope_tables(seq):
    rows = seq // GRID_W
    row = jnp.repeat(jnp.arange(rows, dtype=F32), GRID_W)
    colp = jnp.tile(jnp.arange(GRID_W, dtype=F32), rows)
    n_freq = HEAD_DIM // 4
    inv = ROPE_BASE ** (-jnp.arange(n_freq, dtype=F32) / n_freq)
    ang_r = row[:, None] * inv[None]
    ang_c = colp[:, None] * inv[None]
    cos_t = jnp.concatenate([jnp.cos(ang_r)] * 2 + [jnp.cos(ang_c)] * 2, axis=-1)
    sin_t = jnp.concatenate([-jnp.sin(ang_r), jnp.sin(ang_r), -jnp.sin(ang_c), jnp.sin(ang_c)], axis=-1)
    return cos_t, sin_t


def _row_bcast(x, r):
    return jnp.broadcast_to(x[r:r + 1, :], x.shape)


SCAN_ROWS = SUBLANES * SUBLANES


def _segment_scan(a_ref, b_ref, t0, carry, reverse):
    idx = [pl.ds(t0 + k, SUBLANES, stride=SUBLANES) for k in range(SUBLANES)]
    a = [a_ref[i, :] for i in idx]
    b = [b_ref[i, :] for i in idx]
    order = list(range(SUBLANES - 1, -1, -1)) if reverse else list(range(SUBLANES))
    h = [None] * SUBLANES
    p = [None] * SUBLANES
    prev = None
    for k in order:
        h[k] = b[k] if prev is None else a[k] * h[prev] + b[k]
        p[k] = a[k] if prev is None else a[k] * p[prev]
        prev = k
    pp, hh = p[prev], h[prev]
    sub = lax.broadcasted_iota(jnp.int32, (SUBLANES, LANES), 0)
    for d in (1, 2, 4):
        keep = (sub < SUBLANES - d) if reverse else (sub >= d)
        shift = SUBLANES - d if reverse else d
        p_sh = jnp.where(keep, pltpu.roll(pp, shift, 0), 1.0)
        h_sh = jnp.where(keep, pltpu.roll(hh, shift, 0), 0.0)
        hh = pp * h_sh + hh
        pp = pp * p_sh
    end = pp * carry + hh
    if reverse:
        c_in = jnp.where(sub < SUBLANES - 1, pltpu.roll(end, SUBLANES - 1, 0), carry)
    else:
        c_in = jnp.where(sub >= 1, pltpu.roll(end, 1, 0), carry)
    for k in range(SUBLANES):
        b_ref[idx[k], :] = h[k] + p[k] * c_in
    return _row_bcast(end, 0 if reverse else SUBLANES - 1)


def _lru_kernel(u_ref, z_ref, cw_ref, cb_ref, gw_ref, gb_ref, lam_ref, h0_ref, y_ref, hfin_ref,
                up_ref, af_ref, bf_ref, ab_ref, bb_ref, *, seq, chunk):
    pad = SUBLANES
    up_ref[0:pad, :] = jnp.zeros((pad, LANES), F32)
    up_ref[pad:pad + seq, :] = u_ref[...]
    up_ref[pad + seq:2 * pad + seq, :] = jnp.zeros((pad, LANES), F32)
    nlam = -lam_ref[...]
    softplus = jnp.maximum(nlam, 0.0) + jnp.log1p(jnp.exp(-jnp.abs(nlam)))
    half_rate = (-0.5 * LRU_C) * softplus

    def gates(ci, carry):
        rows = pl.ds(pl.multiple_of(ci * chunk, chunk), chunk)
        xc = cb_ref[...]
        for j in range(CONV_W):
            tap = up_ref[pl.ds(ci * chunk + (pad - CONV_LEFT + j), chunk), :]
            xc = xc + tap * cw_ref[j:j + 1, :]
        g = jnp.dot(xc.astype(BF16), gw_ref[...], preferred_element_type=F32) + gb_ref[...]
        xc_half = 0.5 * xc
        for d, (a_ref, b_ref) in enumerate(((af_ref, bf_ref), (ab_ref, bb_ref))):
            t_r = jnp.tanh(g[:, (2 * d) * LANES:(2 * d + 1) * LANES])
            t_i = jnp.tanh(g[:, (2 * d + 1) * LANES:(2 * d + 2) * LANES])
            log_a = (t_r + 1.0) * half_rate[d:d + 1, :]
            a = jnp.exp(log_a)
            one_minus_a2 = -jnp.tanh(log_a) * (a * a + 1.0)
            a_ref[rows, :] = a
            b_ref[rows, :] = jnp.sqrt(jnp.maximum(one_minus_a2, 0.0)) * ((t_i + 1.0) * xc_half)
        return carry

    lax.fori_loop(0, seq // chunk, gates, 0, unroll=min(4, seq // chunk))

    n_scan = seq // SCAN_ROWS

    def scan(ci, carry):
        c_f, c_b = carry
        c_f = _segment_scan(af_ref, bf_ref, ci * SCAN_ROWS, c_f, False)
        c_b = _segment_scan(ab_ref, bb_ref, (n_scan - 1 - ci) * SCAN_ROWS, c_b, True)
        return c_f, c_b

    h0 = h0_ref[...]
    c_f, c_b = lax.fori_loop(0, n_scan, scan, (jnp.broadcast_to(h0[0:1, :], (SUBLANES, LANES)),
                                               jnp.broadcast_to(h0[1:2, :], (SUBLANES, LANES))),
                             unroll=2)
    hfin_ref[0:1, :] = c_f[0:1, :]
    hfin_ref[1:2, :] = c_b[0:1, :]

    def gate_out(ci, carry):
        rows = pl.ds(pl.multiple_of(ci * chunk, chunk), chunk)
        y_ref[rows, :] = ((bf_ref[rows, :] + bb_ref[rows, :]) * _silu(z_ref[rows, :])).astype(BF16)
        return carry

    lax.fori_loop(0, seq // chunk, gate_out, 0, unroll=2)


def _rglru(mix, conv_w, conv_b, gate_w_b, gate_b, lam, h0, col, chunk):
    bsz, seq, _ = mix.shape
    assert seq % SCAN_ROWS == 0 and seq % chunk == 0
    lw = LRU_BLOCKS * LANES
    ucol = col["u_lru"] // LANES
    zcol = col["z_lru"] // LANES
    return pl.pallas_call(
        functools.partial(_lru_kernel, seq=seq, chunk=chunk),
        grid=(bsz, LRU_BLOCKS),
        in_specs=[
            pl.BlockSpec((None, seq, LANES), lambda b, n: (b, 0, ucol + n)),
            pl.BlockSpec((None, seq, LANES), lambda b, n: (b, 0, zcol + n)),
            pl.BlockSpec((CONV_W, LANES), lambda b, n: (0, n)),
            pl.BlockSpec((1, LANES), lambda b, n: (0, n)),
            pl.BlockSpec((None, LANES, 4 * LANES), lambda b, n: (n, 0, 0)),
            pl.BlockSpec((None, 1, 4 * LANES), lambda b, n: (n, 0, 0)),
            pl.BlockSpec((2, LANES), lambda b, n: (0, n)),
            pl.BlockSpec((None, 2, LANES), lambda b, n: (b, 0, n)),
        ],
        out_specs=[
            pl.BlockSpec((None, seq, LANES), lambda b, n: (b, 0, n)),
            pl.BlockSpec((None, 2, LANES), lambda b, n: (b, 0, n)),
        ],
        out_shape=[
            jax.ShapeDtypeStruct((bsz, seq, lw), BF16),
            jax.ShapeDtypeStruct((bsz, 2, lw), F32),
        ],
        scratch_shapes=[pltpu.VMEM((seq + 2 * SUBLANES, LANES), F32)]
        + [pltpu.VMEM((seq, LANES), F32)] * 4,
        compiler_params=_params(("arbitrary", "arbitrary")),
        name="rglru",
    )(mix, mix, conv_w, conv_b, gate_w_b, gate_b, lam, h0)


def _out_kernel(x_ref, h_ref, mod_ref, gpost_ref, yp_ref, ya_ref, yl_ref,
                wm0_ref, wm1_ref, wm2_ref, wp_ref, wa_ref, wl_ref, wo_ref, o_ref, mg_ref, acc_ref,
                *, n_merge, n_out, tn, tn_out):
    j = pl.program_id(1)

    @pl.when(j < n_merge)
    def _():
        h = h_ref[...]
        merged = None
        for wm_ref, y_ref, wb_ref in ((wm0_ref, yp_ref, wp_ref), (wm1_ref, ya_ref, wa_ref),
                                      (wm2_ref, yl_ref, wl_ref)):
            gate = _sigmoid(jnp.dot(h, wm_ref[...], preferred_element_type=F32))
            term = gate * jnp.dot(y_ref[...], wb_ref[...], preferred_element_type=F32)
            merged = term if merged is None else merged + term
        merged = merged.astype(BF16)
        for c in range(n_merge):
            @pl.when(j == c)
            def _():
                mg_ref[:, c * tn:(c + 1) * tn] = merged

    @pl.when(j == n_merge)
    def _():
        for c in range(n_out):
            cols = slice(c * tn_out, (c + 1) * tn_out)
            acc_ref[:, cols] = jnp.dot(mg_ref[...], wo_ref[:, cols], preferred_element_type=F32)
        gain = gpost_ref[...] * mod_ref[2]

        def body(c, carry):
            rows = pl.ds(pl.multiple_of(c * NORM_ROWS, NORM_ROWS), NORM_ROWS)
            out = acc_ref[rows, :]
            inv = lax.rsqrt(jnp.mean(out * out, axis=-1, keepdims=True) + EPS)
            o_ref[rows, :] = x_ref[rows, :] + (out * inv) * gain
            return carry

        lax.fori_loop(0, o_ref.shape[0] // NORM_ROWS, body, 0, unroll=NORM_UNROLL)


def _out_stage(x2, h2, mod, g_post, y_pool, y_attn, y_lru, w_in_b, w_pool_o, w_attn_o, w_lru_o,
               w_out, n_mix, seq, mod_row0, tm, tn, tn_out):
    rows, d = x2.shape
    n_merge = d // tn
    n_out = d // tn_out
    if mod_row0 is None:
        mod_map = lambda i, j: ((i * tm) // seq, 0, 0, 0)
    else:
        mod_map = lambda i, j: (mod_row0, 0, 0, 0)
    mstep = lambda j: jnp.minimum(j, n_merge - 1)
    mcol = lambda br: (lambda i, j: (0, (n_mix + br * d) // tn + mstep(j)))
    bcol = lambda i, j: (0, mstep(j))
    return pl.pallas_call(
        functools.partial(_out_kernel, n_merge=n_merge, n_out=n_out, tn=tn, tn_out=tn_out),
        grid=(rows // tm, n_merge + 1),
        in_specs=[
            pl.BlockSpec((tm, d), lambda i, j: (i, 0)),
            pl.BlockSpec((tm, d), lambda i, j: (i, 0)),
            pl.BlockSpec((None, 3, 1, d), mod_map),
            pl.BlockSpec((1, d), lambda i, j: (0, 0)),
            pl.BlockSpec((tm, y_pool.shape[1]), lambda i, j: (i, 0)),
            pl.BlockSpec((tm, y_attn.shape[1]), lambda i, j: (i, 0)),
            pl.BlockSpec((tm, y_lru.shape[1]), lambda i, j: (i, 0)),
            pl.BlockSpec((d, tn), mcol(0)),
            pl.BlockSpec((d, tn), mcol(1)),
            pl.BlockSpec((d, tn), mcol(2)),
            pl.BlockSpec((w_pool_o.shape[0], tn), bcol),
            pl.BlockSpec((w_attn_o.shape[0], tn), bcol),
            pl.BlockSpec((w_lru_o.shape[0], tn), bcol),
            pl.BlockSpec((d, d), lambda i, j: (0, 0), pipeline_mode=pl.Buffered(1)),
        ],
        out_specs=pl.BlockSpec((tm, d), lambda i, j: (i, 0)),
        out_shape=jax.ShapeDtypeStruct((rows, d), F32),
        scratch_shapes=[pltpu.VMEM((tm, d), BF16), pltpu.VMEM((tm, d), F32)],
        compiler_params=_params(("arbitrary", "arbitrary")),
        name="out_stage",
    )(x2, h2, mod, g_post, y_pool, y_attn, y_lru, w_in_b, w_in_b, w_in_b,
      w_pool_o, w_attn_o, w_lru_o, w_out)


def _tile(n, want):
    t = min(n, want)
    assert n % t == 0, (n, t)
    return t


def _trunk_layer(x, mod, mod_row0, P, col, n_mix, ctx):
    bsz, seq, d = x.shape
    x2 = x.reshape(bsz * seq, d)
    rows = bsz * seq
    tm_in = _tile(rows if mod_row0 is not None else seq, 512)
    mix, h2 = _in_projection(x2, mod, P["g_pre"], P["w_in"], n_mix, seq, mod_row0, tm_in, 768)
    mix = mix.reshape(bsz, seq, n_mix)
    y_pool = _pool_mixer(mix, P["w_pool_map"], P["pool_scale"], _tile(seq, 512))
    if ctx is None:
        y_attn = _context_attention(mix, P["attn_sink"], col)
        h0 = jnp.zeros((bsz, 2, LRU_BLOCKS * LANES), F32)
    else:
        k_ctx, v_ctx, h0, cos_t, sin_t = ctx
        y_attn = _latent_attention(mix, P["attn_sink"], k_ctx, v_ctx, cos_t, sin_t, col)
    y_lru, h_fin = _rglru(mix, P["lru_conv_w"], P["lru_conv_b"], P["lru_gate_w"], P["lru_gate_b"],
                          P["lru_lambda"], h0, col, _tile(seq, 128))
    tm_out = _tile(rows if mod_row0 is not None else seq, 512)
    x_new = _out_stage(x2, h2, mod, P["g_post"], y_pool.reshape(rows, -1),
                       y_attn.reshape(rows, -1), y_lru.reshape(rows, -1), P["w_in"], P["w_pool_o"],
                       P["w_attn_o"], P["w_lru_o"], P["w_out"], n_mix, seq, mod_row0, tm_out, 512, 512)
    return x_new.reshape(bsz, seq, d), mix, h_fin


def kernel(x_prompt, x_sample, cache_k, cache_v, state_lru, c, c_ctx, g_pre, g_post, w_ada, b_ada,
           w_in, w_pool_map, pool_scale, attn_sink, lru_conv_w, lru_conv_b, lru_gate_w, lru_gate_b,
           lru_lambda, w_pool_o, w_attn_o, w_lru_o, w_out):
    depth = w_in.shape[0]
    d = x_prompt.shape[-1]
    dec_b, dec_seq, _ = x_sample.shape
    pool_w = len(POOL_WINDOWS) * LANES
    attn_w = N_HEADS * HEAD_DIM
    kv_w = N_KV * HEAD_DIM
    lru_w = LRU_BLOCKS * LANES
    widths = (("u_pool", pool_w), ("z_pool", pool_w), ("q", attn_w), ("k", kv_w), ("v", kv_w),
              ("z_attn", attn_w), ("u_lru", lru_w), ("z_lru", lru_w))
    col, n_mix = {}, 0
    for name, w in widths:
        col[name] = n_mix
        n_mix += w
    assert w_in.shape[-1] == n_mix + N_BRANCH * d
    assert dec_b < MOD_ROWS

    cvec = jnp.zeros((MOD_ROWS, d), F32).at[:dec_b].set(c).at[dec_b].set(c_ctx)
    mod = _modulation(cvec, w_ada, b_ada)
    mod = mod.reshape(depth, MOD_ROWS, 3, 1, d)
    cos_t, sin_t = _rope_tables(dec_seq)
    past = cache_k.shape[2]
    cache_k2 = cache_k.reshape(dec_b, depth, past, kv_w)
    cache_v2 = cache_v.reshape(dec_b, depth, past, kv_w)

    xp, xs = x_prompt, x_sample
    new_k, new_v, new_h = [], [], []
    for l in range(depth):
        P = {
            "g_pre": g_pre[l][None], "g_post": g_post[l][None],
            "w_in": w_in[l].astype(BF16),
            "w_pool_map": w_pool_map[l].astype(BF16), "pool_scale": pool_scale[l][None],
            "attn_sink": attn_sink[l],
            "lru_conv_w": lru_conv_w[l], "lru_conv_b": lru_conv_b[l][None],
            "lru_gate_w": (0.5 * lru_gate_w[l]).transpose(2, 3, 0, 1, 4).reshape(
                LRU_BLOCKS, LANES, 4 * LANES).astype(BF16),
            "lru_gate_b": (0.5 * lru_gate_b[l]).reshape(2, 2, LRU_BLOCKS, LANES).transpose(
                2, 0, 1, 3).reshape(LRU_BLOCKS, 1, 4 * LANES),
            "lru_lambda": lru_lambda[l],
            "w_pool_o": w_pool_o[l].astype(BF16), "w_attn_o": w_attn_o[l].astype(BF16),
            "w_lru_o": w_lru_o[l].astype(BF16), "w_out": w_out[l].astype(BF16),
        }
        xp, mix_p, h_fin = _trunk_layer(xp, mod[l], dec_b, P, col, n_mix, None)
        bsz, seq = xp.shape[:2]
        new_k.append(mix_p[:, :, col["k"]:col["k"] + kv_w].reshape(bsz, seq, N_KV, HEAD_DIM))
        new_v.append(mix_p[:, :, col["v"]:col["v"] + kv_w].reshape(bsz, seq, N_KV, HEAD_DIM))
        new_h.append(h_fin)
        ctx = (cache_k2[:, l], cache_v2[:, l], state_lru[:, l], cos_t, sin_t)
        xs, _, _ = _trunk_layer(xs, mod[l], None, P, col, n_mix, ctx)
    return (xp, xs, jnp.stack(new_k, axis=1), jnp.stack(new_v, axis=1), jnp.stack(new_h, axis=1))
```

```python
import functools

import jax
import jax.numpy as jnp
from jax import lax
from jax.experimental import pallas as pl
from jax.experimental.pallas import tpu as pltpu

F32 = jnp.float32
BF16 = jnp.bfloat16

EPS = 1e-6
LANES = 128
SUBLANES = 8
POOL_WINDOWS = (2, 4, 8, 16)
N_HEADS = 8
N_KV = 2
HEAD_DIM = 128
GQA_G = N_HEADS // N_KV
WINDOW = 128
BLOCK = 128
GRID_W = 64
ROPE_BASE = 10000.0
LRU_BLOCKS = 4
CONV_W = 4
CONV_LEFT = 2
LRU_C = 8.0
N_BRANCH = 3
MOD_ROWS = 16

VMEM_LIMIT = 60 * 1024 * 1024


def _sigmoid(x):
    return 0.5 * (jnp.tanh(0.5 * x) + 1.0)


def _silu(x):
    return x * _sigmoid(x)


def _params(sem):
    return pltpu.CompilerParams(dimension_semantics=sem, vmem_limit_bytes=VMEM_LIMIT)


def _mod_kernel(c_ref, w_ref, b_ref, o_ref):
    s = _silu(c_ref[...]).astype(BF16)
    o_ref[...] = jnp.dot(s, w_ref[...].astype(BF16), preferred_element_type=F32) + b_ref[...]


def _modulation(cvec, w_ada, b_ada):
    depth, d, n = w_ada.shape
    tn = 768
    return pl.pallas_call(
        _mod_kernel,
        grid=(depth, n // tn),
        in_specs=[
            pl.BlockSpec((MOD_ROWS, d), lambda l, j: (0, 0)),
            pl.BlockSpec((None, d, tn), lambda l, j: (l, 0, j)),
            pl.BlockSpec((None, 1, tn), lambda l, j: (l, 0, j)),
        ],
        out_specs=pl.BlockSpec((None, MOD_ROWS, tn), lambda l, j: (l, 0, j)),
        out_shape=jax.ShapeDtypeStruct((depth, MOD_ROWS, n), F32),
        compiler_params=_params(("arbitrary", "arbitrary")),
        name="adaln_mod",
    )(cvec, w_ada, b_ada.reshape(depth, 1, n))


NORM_ROWS = 16
NORM_UNROLL = 8


def _prenorm_into(h_ref, x_ref, g_ref, mod_ref):
    gain = g_ref[...] * (1.0 + mod_ref[1])
    shift = mod_ref[0]

    def body(c, carry):
        rows = pl.ds(pl.multiple_of(c * NORM_ROWS, NORM_ROWS), NORM_ROWS)
        x = x_ref[rows, :]
        inv = lax.rsqrt(jnp.mean(x * x, axis=-1, keepdims=True) + EPS)
        h_ref[rows, :] = ((x * inv) * gain + shift).astype(BF16)
        return carry

    lax.fori_loop(0, x_ref.shape[0] // NORM_ROWS, body, 0, unroll=NORM_UNROLL)


def _inproj_kernel(x_ref, mod_ref, g_ref, w_ref, *rest, tn, kinds, rope):
    if rope:
        cos_ref, sin_ref, o_ref, h_ref = rest
        cos_k, sin_k = cos_ref[...], sin_ref[...]
        q_scale = HEAD_DIM ** -0.5 * LOG2E
        cos_q, sin_q = cos_k * q_scale, sin_k * q_scale
    else:
        o_ref, h_ref = rest
    _prenorm_into(h_ref, x_ref, g_ref, mod_ref)
    slabs = tn // LANES
    for c in range(o_ref.shape[1] // tn):
        cols = slice(c * tn, (c + 1) * tn)
        res = jnp.dot(h_ref[...], w_ref[:, cols], preferred_element_type=F32)
        parts = []
        for s in range(slabs):
            part = res[:, s * LANES:(s + 1) * LANES]
            kind = kinds[c * slabs + s]
            if kind == "z":
                part = _silu(part)
            elif kind == "q" and rope:
                part = _rope(part, cos_q, sin_q)
            elif kind == "k" and rope:
                part = _rope(part, cos_k, sin_k)
            parts.append(part)
        o_ref[:, cols] = jnp.concatenate(parts, axis=1)


def _in_projection(x2, mod, g_pre, w_in_b, n_mix, seq, mod_row0, tm, tn, kinds, tables):
    rows, d = x2.shape
    if mod_row0 is None:
        mod_map = lambda i: ((i * tm) // seq, 0, 0, 0)
    else:
        mod_map = lambda i: (mod_row0, 0, 0, 0)
    in_specs = [
        pl.BlockSpec((tm, d), lambda i: (i, 0)),
        pl.BlockSpec((None, 3, 1, d), mod_map),
        pl.BlockSpec((1, d), lambda i: (0, 0)),
        pl.BlockSpec((d, n_mix), lambda i: (0, 0), pipeline_mode=pl.Buffered(1)),
    ]
    operands = [x2, mod, g_pre, w_in_b]
    if tables is not None:
        per_seq = seq // tm
        in_specs += [pl.BlockSpec((tm, HEAD_DIM), lambda i: (i % per_seq, 0))] * 2
        operands += list(tables)
    return pl.pallas_call(
        functools.partial(_inproj_kernel, tn=tn, kinds=kinds, rope=tables is not None),
        grid=(rows // tm,),
        in_specs=in_specs,
        out_specs=[pl.BlockSpec((tm, n_mix), lambda i: (i, 0)),
                   pl.BlockSpec((tm, d), lambda i: (i, 0))],
        out_shape=[jax.ShapeDtypeStruct((rows, n_mix), F32),
                   jax.ShapeDtypeStruct((rows, d), BF16)],
        compiler_params=_params(("arbitrary",)),
        name="in_proj",
    )(*operands)


POOL_EDGE = 16


def _window_sum(ext_ref, pa_ref, pb_ref, lanes, base, w, tt):
    if w <= 4:
        acc = ext_ref[base:base + tt, lanes]
        for k in range(1, w):
            acc = acc + ext_ref[base + k:base + k + tt, lanes]
        return acc
    n = tt + w - 2
    pa_ref[0:n, :] = ext_ref[base:base + n, lanes] + ext_ref[base + 1:base + 1 + n, lanes]
    src, dst, s = pa_ref, pb_ref, 2
    while 2 * s < w:
        n = tt + w - 2 * s
        dst[0:n, :] = src[0:n, :] + src[s:s + n, :]
        src, dst, s = dst, src, 2 * s
    return src[0:tt, :] + src[s:s + tt, :]


def _pool_kernel(up_ref, uc_ref, un_ref, z_ref, wm_ref, sc_ref, o_ref, ext_ref, pa_ref, pb_ref,
                 d_ref, *, seq, tt):
    i = pl.program_id(1)
    nt = pl.num_programs(1)
    halo = SUBLANES
    ext_ref[0:halo, :] = jnp.where(i > 0, up_ref[...], 0.0)
    ext_ref[halo:halo + tt, :] = uc_ref[...]
    ext_ref[halo + tt:2 * halo + tt, :] = jnp.where(i < nt - 1, un_ref[...], 0.0)
    for g, w in enumerate(POOL_WINDOWS):
        lanes = slice(g * LANES, (g + 1) * LANES)
        acc = _window_sum(ext_ref, pa_ref, pb_ref, lanes, halo - w // 2, w, tt)
        d_ref[:, lanes] = (acc * (1.0 / w) - uc_ref[:, lanes]).astype(BF16)
        for r0 in (0, tt - POOL_EDGE):
            t = i * tt + r0 + lax.broadcasted_iota(jnp.int32, (POOL_EDGE, 1), 0)
            cnt = (jnp.minimum(t + w // 2, seq) - jnp.maximum(t - w // 2, 0)).astype(F32)
            edge = acc[r0:r0 + POOL_EDGE, :] / cnt - uc_ref[r0:r0 + POOL_EDGE, lanes]
            d_ref[r0:r0 + POOL_EDGE, lanes] = edge.astype(BF16)
        y = jnp.dot(d_ref[:, lanes], wm_ref[g], preferred_element_type=F32)
        y = (y * sc_ref[:, lanes]) * z_ref[:, lanes]
        o_ref[:, lanes] = y.astype(BF16)


def _pool_mixer(mix, w_map_b, pool_scale, tt):
    bsz, seq, _ = mix.shape
    pw = len(POOL_WINDOWS) * LANES
    nh = tt // SUBLANES
    last = seq // SUBLANES - 1
    return pl.pallas_call(
        functools.partial(_pool_kernel, seq=seq, tt=tt),
        grid=(bsz, seq // tt),
        in_specs=[
            pl.BlockSpec((None, SUBLANES, pw), lambda b, i: (b, jnp.maximum(i * nh - 1, 0), 0)),
            pl.BlockSpec((None, tt, pw), lambda b, i: (b, i, 0)),
            pl.BlockSpec((None, SUBLANES, pw), lambda b, i: (b, jnp.minimum((i + 1) * nh, last), 0)),
            pl.BlockSpec((None, tt, pw), lambda b, i: (b, i, 1)),
            pl.BlockSpec((len(POOL_WINDOWS), LANES, LANES), lambda b, i: (0, 0, 0)),
            pl.BlockSpec((1, pw), lambda b, i: (0, 0)),
        ],
        out_specs=pl.BlockSpec((None, tt, pw), lambda b, i: (b, i, 0)),
        out_shape=jax.ShapeDtypeStruct((bsz, seq, pw), BF16),
        scratch_shapes=[pltpu.VMEM((tt + 2 * SUBLANES, pw), F32),
                        pltpu.VMEM((tt + 2 * SUBLANES, LANES), F32),
                        pltpu.VMEM((tt + 2 * SUBLANES, LANES), F32),
                        pltpu.VMEM((tt, pw), BF16)],
        compiler_params=_params(("arbitrary", "arbitrary")),
        name="pool_mixer",
    )(mix, mix, mix, mix, w_map_b, pool_scale)


def _stack_heads(x):
    return jnp.concatenate([x[:, g * HEAD_DIM:(g + 1) * HEAD_DIM] for g in range(GQA_G)], axis=0)


def _unstack_heads(x, rows):
    return jnp.concatenate([x[g * rows:(g + 1) * rows, :] for g in range(GQA_G)], axis=1)


def _sink_column(sink_ref, kv, rows):
    head = lax.broadcasted_iota(jnp.int32, (GQA_G * rows, 1), 0) // rows
    col = jnp.full((GQA_G * rows, 1), sink_ref[kv * GQA_G], F32)
    for g in range(1, GQA_G):
        col = jnp.where(head == g, sink_ref[kv * GQA_G + g], col)
    return col


LOG2E = 1.4426950408889634

def _softmax_pv(s, sink, v):
    m = jnp.maximum(jnp.max(s, axis=-1, keepdims=True), sink)
    e = jnp.exp2(s - m)
    den = jnp.sum(e, axis=-1, keepdims=True) + jnp.exp2(sink - m)
    o = jnp.dot(e.astype(BF16), v, preferred_element_type=F32)
    return o / den


def _qk(q, k):
    return lax.dot_general(q, k, (((1,), (1,)), ((), ())), preferred_element_type=F32)


def _ctx_attn_kernel(sink_ref, q_ref, k_ref, v_ref, z_ref, o_ref, *, seq):
    kv = pl.program_id(1)
    scale = HEAD_DIM ** -0.5 * LOG2E
    q = _stack_heads(q_ref[...]).astype(BF16)
    s = _qk(q, k_ref[...].astype(BF16)) * scale
    o = _softmax_pv(s, _sink_column(sink_ref, kv, seq) * LOG2E, v_ref[...].astype(BF16))
    o_ref[...] = (_unstack_heads(o, seq) * z_ref[...]).astype(BF16)


def _context_attention(mix, sink, col):
    bsz, seq, _ = mix.shape
    gw = GQA_G * HEAD_DIM
    return pl.pallas_call(
        functools.partial(_ctx_attn_kernel, seq=seq),
        grid=(bsz, N_KV),
        in_specs=[
            pl.BlockSpec(memory_space=pltpu.SMEM),
            pl.BlockSpec((None, seq, gw), lambda b, kv: (b, 0, col["q"] // gw + kv)),
            pl.BlockSpec((None, seq, HEAD_DIM), lambda b, kv: (b, 0, col["k"] // HEAD_DIM + kv)),
            pl.BlockSpec((None, seq, HEAD_DIM), lambda b, kv: (b, 0, col["v"] // HEAD_DIM + kv)),
            pl.BlockSpec((None, seq, gw), lambda b, kv: (b, 0, col["z_attn"] // gw + kv)),
        ],
        out_specs=pl.BlockSpec((None, seq, gw), lambda b, kv: (b, 0, kv)),
        out_shape=jax.ShapeDtypeStruct((bsz, seq, N_KV * gw), BF16),
        compiler_params=_params(("arbitrary", "arbitrary")),
        name="ctx_attention",
    )(sink, mix, mix, mix, mix)


def _rope(x, cos, sin):
    lane = lax.broadcasted_iota(jnp.int32, x.shape, 1)
    quarter = HEAD_DIM // 4
    first = (lane % (2 * quarter)) < quarter
    partner = jnp.where(first, pltpu.roll(x, HEAD_DIM - quarter, 1), pltpu.roll(x, quarter, 1))
    return x * cos + partner * sin


def _lat_attn_kernel(sink_ref, q_ref, k_ref, v_ref, kx_ref, vx_ref, z_ref, o_ref,
                     kr_ref, vb_ref, bias_ref, *, seq, qc):
    kv = pl.program_id(1)
    ci = pl.program_id(2)
    nb = seq // BLOCK
    nb_c = qc // BLOCK
    span = 3 * BLOCK

    @pl.when(ci == 0)
    def _():
        zeros = jnp.zeros((BLOCK, HEAD_DIM), BF16)
        for ref in (kr_ref, vb_ref):
            ref[0:BLOCK, :] = zeros
            ref[BLOCK + seq:2 * BLOCK + seq, :] = zeros

        def fill(i, carry):
            rows = pl.ds(pl.multiple_of(i * BLOCK, BLOCK), BLOCK)
            dst = pl.ds(pl.multiple_of((i + 1) * BLOCK, BLOCK), BLOCK)
            kr_ref[dst, :] = k_ref[rows, :].astype(BF16)
            vb_ref[dst, :] = v_ref[rows, :].astype(BF16)
            return carry

        lax.fori_loop(0, nb, fill, 0, unroll=4)
        r = lax.broadcasted_iota(jnp.int32, (GQA_G * BLOCK, span), 0) % BLOCK
        c = lax.broadcasted_iota(jnp.int32, (GQA_G * BLOCK, span), 1)
        band = (c >= r) & (c <= r + 2 * WINDOW)
        neg = jnp.float32(-jnp.inf)
        bias_ref[0] = jnp.where(band & (c >= BLOCK), 0.0, neg)
        bias_ref[1] = jnp.where(band, 0.0, neg)
        bias_ref[2] = jnp.where(band & (c < 2 * BLOCK), 0.0, neg)

    sink = _sink_column(sink_ref, kv, BLOCK) * LOG2E
    kx = kx_ref[...].astype(BF16)
    vx = vx_ref[...].astype(BF16)

    def block(jb, carry):
        bi = ci * nb_c + jb
        rows = pl.ds(pl.multiple_of(jb * BLOCK, BLOCK), BLOCK)
        q = _stack_heads(q_ref[rows, :]).astype(BF16)
        win = pl.ds(pl.multiple_of(bi * BLOCK, BLOCK), span)
        variant = jnp.where(bi == 0, 0, jnp.where(bi == nb - 1, 2, 1))
        s = jnp.concatenate([_qk(q, kr_ref[win, :]) + bias_ref[variant], _qk(q, kx)], axis=1)
        m = jnp.maximum(jnp.max(s, axis=-1, keepdims=True), sink)
        e = jnp.exp2(s - m)
        den = jnp.sum(e, axis=-1, keepdims=True) + jnp.exp2(sink - m)
        e = e.astype(BF16)
        o = (jnp.dot(e[:, :span], vb_ref[win, :], preferred_element_type=F32)
             + jnp.dot(e[:, span:], vx, preferred_element_type=F32)) / den
        o_ref[rows, :] = (_unstack_heads(o, BLOCK) * z_ref[rows, :]).astype(BF16)
        return carry

    lax.fori_loop(0, nb_c, block, 0, unroll=min(4, nb_c))


def _latent_attention(mix, sink, k_ctx, v_ctx, col):
    bsz, seq, _ = mix.shape
    tc = k_ctx.shape[1]
    assert seq // BLOCK >= 2
    qc = _tile(seq, 1024)
    gw = GQA_G * HEAD_DIM
    kcol = col["k"] // HEAD_DIM
    vcol = col["v"] // HEAD_DIM
    return pl.pallas_call(
        functools.partial(_lat_attn_kernel, seq=seq, qc=qc),
        grid=(bsz, N_KV, seq // qc),
        in_specs=[
            pl.BlockSpec(memory_space=pltpu.SMEM),
            pl.BlockSpec((None, qc, gw), lambda b, kv, ci: (b, ci, col["q"] // gw + kv)),
            pl.BlockSpec((None, seq, HEAD_DIM), lambda b, kv, ci: (b, 0, kcol + kv)),
            pl.BlockSpec((None, seq, HEAD_DIM), lambda b, kv, ci: (b, 0, vcol + kv)),
            pl.BlockSpec((None, tc, HEAD_DIM), lambda b, kv, ci: (b, 0, kv)),
            pl.BlockSpec((None, tc, HEAD_DIM), lambda b, kv, ci: (b, 0, kv)),
            pl.BlockSpec((None, qc, gw), lambda b, kv, ci: (b, ci, col["z_attn"] // gw + kv)),
        ],
        out_specs=pl.BlockSpec((None, qc, gw), lambda b, kv, ci: (b, ci, kv)),
        out_shape=jax.ShapeDtypeStruct((bsz, seq, N_KV * gw), BF16),
        scratch_shapes=[
            pltpu.VMEM((seq + 2 * BLOCK, HEAD_DIM), BF16),
            pltpu.VMEM((seq + 2 * BLOCK, HEAD_DIM), BF16),
            pltpu.VMEM((3, GQA_G * BLOCK, 3 * BLOCK), F32),
        ],
        compiler_params=_params(("arbitrary", "arbitrary", "arbitrary")),
        name="latent_attention",
    )(sink, mix, mix, mix, k_ctx, v_ctx, mix)


def _rope_tables(seq):
    rows = seq // GRID_W
    row = jnp.repeat(jnp.arange(rows, dtype=F32), GRID_W)
    colp = jnp.tile(jnp.arange(GRID_W, dtype=F32), rows)
    n_freq = HEAD_DIM // 4
    inv = ROPE_BASE ** (-jnp.arange(n_freq, dtype=F32) / n_freq)
    ang_r = row[:, None] * inv[None]
    ang_c = colp[:, None] * inv[None]
    cos_t = jnp.concatenate([jnp.cos(ang_r)] * 2 + [jnp.cos(ang_c)] * 2, axis=-1)
    sin_t = jnp.concatenate([-jnp.sin(ang_r), jnp.sin(ang_r), -jnp.sin(ang_c), jnp.sin(ang_c)], axis=-1)
    return cos_t, sin_t


def _row_bcast(x, r):
    return jnp.broadcast_to(x[r:r + 1, :], x.shape)


SCAN_ROWS = SUBLANES * SUBLANES


def _segment_scan(a_ref, b_ref, t0, carry, reverse):
    idx = [pl.ds(t0 + k, SUBLANES, stride=SUBLANES) for k in range(SUBLANES)]
    a = [a_ref[i, :] for i in idx]
    b = [b_ref[i, :] for i in idx]
    order = list(range(SUBLANES - 1, -1, -1)) if reverse else list(range(SUBLANES))
    h = [None] * SUBLANES
    p = [None] * SUBLANES
    prev = None
    for k in order:
        h[k] = b[k] if prev is None else a[k] * h[prev] + b[k]
        p[k] = a[k] if prev is None else a[k] * p[prev]
        prev = k
    pp, hh = p[prev], h[prev]
    sub = lax.broadcasted_iota(jnp.int32, (SUBLANES, LANES), 0)
    for d in (1, 2, 4):
        keep = (sub < SUBLANES - d) if reverse else (sub >= d)
        shift = SUBLANES - d if reverse else d
        p_sh = jnp.where(keep, pltpu.roll(pp, shift, 0), 1.0)
        h_sh = jnp.where(keep, pltpu.roll(hh, shift, 0), 0.0)
        hh = pp * h_sh + hh
        pp = pp * p_sh
    end = pp * carry + hh
    if reverse:
        c_in = jnp.where(sub < SUBLANES - 1, pltpu.roll(end, SUBLANES - 1, 0), carry)
    else:
        c_in = jnp.where(sub >= 1, pltpu.roll(end, 1, 0), carry)
    for k in range(SUBLANES):
        b_ref[idx[k], :] = h[k] + p[k] * c_in
    return _row_bcast(end, 0 if reverse else SUBLANES - 1)


def _lru_kernel(u_ref, z_ref, cw_ref, cb_ref, gw_ref, gb_ref, lam_ref, h0_ref, y_ref, hfin_ref,
                up_ref, af_ref, bf_ref, ab_ref, bb_ref, *, seq, chunk):
    pad = SUBLANES
    up_ref[0:pad, :] = jnp.zeros((pad, LANES), F32)
    up_ref[pad:pad + seq, :] = u_ref[...]
    up_ref[pad + seq:2 * pad + seq, :] = jnp.zeros((pad, LANES), F32)
    nlam = -lam_ref[...]
    softplus = jnp.maximum(nlam, 0.0) + jnp.log1p(jnp.exp(-jnp.abs(nlam)))
    half_rate = (-0.5 * LRU_C) * softplus

    def gates(ci, carry):
        rows = pl.ds(pl.multiple_of(ci * chunk, chunk), chunk)
        xc = cb_ref[...]
        for j in range(CONV_W):
            tap = up_ref[pl.ds(ci * chunk + (pad - CONV_LEFT + j), chunk), :]
            xc = xc + tap * cw_ref[j:j + 1, :]
        g = jnp.dot(xc.astype(BF16), gw_ref[...], preferred_element_type=F32) + gb_ref[...]
        xc_half = 0.5 * xc
        for d, (a_ref, b_ref) in enumerate(((af_ref, bf_ref), (ab_ref, bb_ref))):
            t_r = jnp.tanh(g[:, (2 * d) * LANES:(2 * d + 1) * LANES])
            t_i = jnp.tanh(g[:, (2 * d + 1) * LANES:(2 * d + 2) * LANES])
            log_a = (t_r + 1.0) * half_rate[d:d + 1, :]
            a = jnp.exp(log_a)
            one_minus_a2 = -jnp.tanh(log_a) * (a * a + 1.0)
            a_ref[rows, :] = a
            b_ref[rows, :] = jnp.sqrt(jnp.maximum(one_minus_a2, 0.0)) * ((t_i + 1.0) * xc_half)
        return carry

    lax.fori_loop(0, seq // chunk, gates, 0, unroll=min(4, seq // chunk))

    n_scan = seq // SCAN_ROWS

    def scan(ci, carry):
        c_f, c_b = carry
        c_f = _segment_scan(af_ref, bf_ref, ci * SCAN_ROWS, c_f, False)
        c_b = _segment_scan(ab_ref, bb_ref, (n_scan - 1 - ci) * SCAN_ROWS, c_b, True)
        return c_f, c_b

    h0 = h0_ref[...]
    c_f, c_b = lax.fori_loop(0, n_scan, scan, (jnp.broadcast_to(h0[0:1, :], (SUBLANES, LANES)),
                                               jnp.broadcast_to(h0[1:2, :], (SUBLANES, LANES))),
                             unroll=2)
    hfin_ref[0:1, :] = c_f[0:1, :]
    hfin_ref[1:2, :] = c_b[0:1, :]

    def gate_out(ci, carry):
        rows = pl.ds(pl.multiple_of(ci * chunk, chunk), chunk)
        y_ref[rows, :] = ((bf_ref[rows, :] + bb_ref[rows, :]) * z_ref[rows, :]).astype(BF16)
        return carry

    lax.fori_loop(0, seq // chunk, gate_out, 0, unroll=2)


def _rglru(mix, conv_w, conv_b, gate_w_b, gate_b, lam, h0, col, chunk):
    bsz, seq, _ = mix.shape
    assert seq % SCAN_ROWS == 0 and seq % chunk == 0
    lw = LRU_BLOCKS * LANES
    ucol = col["u_lru"] // LANES
    zcol = col["z_lru"] // LANES
    return pl.pallas_call(
        functools.partial(_lru_kernel, seq=seq, chunk=chunk),
        grid=(bsz, LRU_BLOCKS),
        in_specs=[
            pl.BlockSpec((None, seq, LANES), lambda b, n: (b, 0, ucol + n)),
            pl.BlockSpec((None, seq, LANES), lambda b, n: (b, 0, zcol + n)),
            pl.BlockSpec((CONV_W, LANES), lambda b, n: (0, n)),
            pl.BlockSpec((1, LANES), lambda b, n: (0, n)),
            pl.BlockSpec((None, LANES, 4 * LANES), lambda b, n: (n, 0, 0)),
            pl.BlockSpec((None, 1, 4 * LANES), lambda b, n: (n, 0, 0)),
            pl.BlockSpec((2, LANES), lambda b, n: (0, n)),
            pl.BlockSpec((None, 2, LANES), lambda b, n: (b, 0, n)),
        ],
        out_specs=[
            pl.BlockSpec((None, seq, LANES), lambda b, n: (b, 0, n)),
            pl.BlockSpec((None, 2, LANES), lambda b, n: (b, 0, n)),
        ],
        out_shape=[
            jax.ShapeDtypeStruct((bsz, seq, lw), BF16),
            jax.ShapeDtypeStruct((bsz, 2, lw), F32),
        ],
        scratch_shapes=[pltpu.VMEM((seq + 2 * SUBLANES, LANES), F32)]
        + [pltpu.VMEM((seq, LANES), F32)] * 4,
        compiler_params=_params(("arbitrary", "arbitrary")),
        name="rglru",
    )(mix, mix, conv_w, conv_b, gate_w_b, gate_b, lam, h0)


def _out_kernel(x_ref, h_ref, mod_ref, gpost_ref, yp_ref, ya_ref, yl_ref,
                wm0_ref, wm1_ref, wm2_ref, wp_ref, wa_ref, wl_ref, wo_ref, o_ref, mg_ref, acc_ref,
                *, n_merge, n_out, tn, tn_out):
    j = pl.program_id(1)

    @pl.when(j < n_merge)
    def _():
        h = h_ref[...]
        merged = None
        for wm_ref, y_ref, wb_ref in ((wm0_ref, yp_ref, wp_ref), (wm1_ref, ya_ref, wa_ref),
                                      (wm2_ref, yl_ref, wl_ref)):
            gate = _sigmoid(jnp.dot(h, wm_ref[...], preferred_element_type=F32))
            term = gate * jnp.dot(y_ref[...], wb_ref[...], preferred_element_type=F32)
            merged = term if merged is None else merged + term
        merged = merged.astype(BF16)
        for c in range(n_merge):
            @pl.when(j == c)
            def _():
                mg_ref[:, c * tn:(c + 1) * tn] = merged

    @pl.when(j == n_merge)
    def _():
        for c in range(n_out):
            cols = slice(c * tn_out, (c + 1) * tn_out)
            acc_ref[:, cols] = jnp.dot(mg_ref[...], wo_ref[:, cols], preferred_element_type=F32)
        gain = gpost_ref[...] * mod_ref[2]

        def body(c, carry):
            rows = pl.ds(pl.multiple_of(c * NORM_ROWS, NORM_ROWS), NORM_ROWS)
            out = acc_ref[rows, :]
            inv = lax.rsqrt(jnp.mean(out * out, axis=-1, keepdims=True) + EPS)
            o_ref[rows, :] = x_ref[rows, :] + (out * inv) * gain
            return carry

        lax.fori_loop(0, o_ref.shape[0] // NORM_ROWS, body, 0, unroll=NORM_UNROLL)


def _out_stage(x2, h2, mod, g_post, y_pool, y_attn, y_lru, w_in_b, w_pool_o, w_attn_o, w_lru_o,
               w_out, n_mix, seq, mod_row0, tm, tn, tn_out):
    rows, d = x2.shape
    n_merge = d // tn
    n_out = d // tn_out
    if mod_row0 is None:
        mod_map = lambda i, j: ((i * tm) // seq, 0, 0, 0)
    else:
        mod_map = lambda i, j: (mod_row0, 0, 0, 0)
    mstep = lambda j: jnp.minimum(j, n_merge - 1)
    mcol = lambda br: (lambda i, j: (0, (n_mix + br * d) // tn + mstep(j)))
    bcol = lambda i, j: (0, mstep(j))
    return pl.pallas_call(
        functools.partial(_out_kernel, n_merge=n_merge, n_out=n_out, tn=tn, tn_out=tn_out),
        grid=(rows // tm, n_merge + 1),
        in_specs=[
            pl.BlockSpec((tm, d), lambda i, j: (i, 0)),
            pl.BlockSpec((tm, d), lambda i, j: (i, 0)),
            pl.BlockSpec((None, 3, 1, d), mod_map),
            pl.BlockSpec((1, d), lambda i, j: (0, 0)),
            pl.BlockSpec((tm, y_pool.shape[1]), lambda i, j: (i, 0)),
            pl.BlockSpec((tm, y_attn.shape[1]), lambda i, j: (i, 0)),
            pl.BlockSpec((tm, y_lru.shape[1]), lambda i, j: (i, 0)),
            pl.BlockSpec((d, tn), mcol(0)),
            pl.BlockSpec((d, tn), mcol(1)),
            pl.BlockSpec((d, tn), mcol(2)),
            pl.BlockSpec((w_pool_o.shape[0], tn), bcol),
            pl.BlockSpec((w_attn_o.shape[0], tn), bcol),
            pl.BlockSpec((w_lru_o.shape[0], tn), bcol),
            pl.BlockSpec((d, d), lambda i, j: (0, 0), pipeline_mode=pl.Buffered(1)),
        ],
        out_specs=pl.BlockSpec((tm, d), lambda i, j: (i, 0)),
        out_shape=jax.ShapeDtypeStruct((rows, d), F32),
        scratch_shapes=[pltpu.VMEM((tm, d), BF16), pltpu.VMEM((tm, d), F32)],
        compiler_params=_params(("arbitrary", "arbitrary")),
        name="out_stage",
    )(x2, h2, mod, g_post, y_pool, y_attn, y_lru, w_in_b, w_in_b, w_in_b,
      w_pool_o, w_attn_o, w_lru_o, w_out)


def _tile(n, want):
    t = min(n, want)
    assert n % t == 0, (n, t)
    return t


def _trunk_layer(x, mod, mod_row0, P, col, kinds, n_mix, ctx):
    bsz, seq, d = x.shape
    x2 = x.reshape(bsz * seq, d)
    rows = bsz * seq
    tm_in = _tile(rows if mod_row0 is not None else seq, 512)
    tables = None if ctx is None else ctx[3:]
    mix, h2 = _in_projection(x2, mod, P["g_pre"], P["w_in"], n_mix, seq, mod_row0, tm_in, 768,
                             kinds, tables)
    mix = mix.reshape(bsz, seq, n_mix)
    y_pool = _pool_mixer(mix, P["w_pool_map"], P["pool_scale"], _tile(seq, 512))
    if ctx is None:
        y_attn = _context_attention(mix, P["attn_sink"], col)
        h0 = jnp.zeros((bsz, 2, LRU_BLOCKS * LANES), F32)
    else:
        k_ctx, v_ctx, h0 = ctx[:3]
        y_attn = _latent_attention(mix, P["attn_sink"], k_ctx, v_ctx, col)
    y_lru, h_fin = _rglru(mix, P["lru_conv_w"], P["lru_conv_b"], P["lru_gate_w"], P["lru_gate_b"],
                          P["lru_lambda"], h0, col, _tile(seq, 128))
    tm_out = _tile(rows if mod_row0 is not None else seq, 512)
    x_new = _out_stage(x2, h2, mod, P["g_post"], y_pool.reshape(rows, -1),
                       y_attn.reshape(rows, -1), y_lru.reshape(rows, -1), P["w_in"], P["w_pool_o"],
                       P["w_attn_o"], P["w_lru_o"], P["w_out"], n_mix, seq, mod_row0, tm_out, 512, 512)
    return x_new.reshape(bsz, seq, d), mix, h_fin


def kernel(x_prompt, x_sample, cache_k, cache_v, state_lru, c, c_ctx, g_pre, g_post, w_ada, b_ada,
           w_in, w_pool_map, pool_scale, attn_sink, lru_conv_w, lru_conv_b, lru_gate_w, lru_gate_b,
           lru_lambda, w_pool_o, w_attn_o, w_lru_o, w_out):
    depth = w_in.shape[0]
    d = x_prompt.shape[-1]
    dec_b, dec_seq, _ = x_sample.shape
    pool_w = len(POOL_WINDOWS) * LANES
    attn_w = N_HEADS * HEAD_DIM
    kv_w = N_KV * HEAD_DIM
    lru_w = LRU_BLOCKS * LANES
    widths = (("u_pool", pool_w), ("z_pool", pool_w), ("q", attn_w), ("k", kv_w), ("v", kv_w),
              ("z_attn", attn_w), ("u_lru", lru_w), ("z_lru", lru_w))
    col, n_mix, kinds = {}, 0, ()
    for name, w in widths:
        col[name] = n_mix
        n_mix += w
        kinds += (name[0] if name[0] in "zqk" else "",) * (w // LANES)
    assert w_in.shape[-1] == n_mix + N_BRANCH * d
    assert dec_b < MOD_ROWS

    cvec = jnp.zeros((MOD_ROWS, d), F32).at[:dec_b].set(c).at[dec_b].set(c_ctx)
    mod = _modulation(cvec, w_ada, b_ada)
    mod = mod.reshape(depth, MOD_ROWS, 3, 1, d)
    cos_t, sin_t = _rope_tables(dec_seq)
    past = cache_k.shape[2]
    cache_k2 = cache_k.reshape(dec_b, depth, past, kv_w)
    cache_v2 = cache_v.reshape(dec_b, depth, past, kv_w)

    xp, xs = x_prompt, x_sample
    new_k, new_v, new_h = [], [], []
    for l in range(depth):
        P = {
            "g_pre": g_pre[l][None], "g_post": g_post[l][None],
            "w_in": w_in[l].astype(BF16),
            "w_pool_map": w_pool_map[l].astype(BF16), "pool_scale": pool_scale[l][None],
            "attn_sink": attn_sink[l],
            "lru_conv_w": lru_conv_w[l], "lru_conv_b": lru_conv_b[l][None],
            "lru_gate_w": (0.5 * lru_gate_w[l]).transpose(2, 3, 0, 1, 4).reshape(
                LRU_BLOCKS, LANES, 4 * LANES).astype(BF16),
            "lru_gate_b": (0.5 * lru_gate_b[l]).reshape(2, 2, LRU_BLOCKS, LANES).transpose(
                2, 0, 1, 3).reshape(LRU_BLOCKS, 1, 4 * LANES),
            "lru_lambda": lru_lambda[l],
            "w_pool_o": w_pool_o[l].astype(BF16), "w_attn_o": w_attn_o[l].astype(BF16),
            "w_lru_o": w_lru_o[l].astype(BF16), "w_out": w_out[l].astype(BF16),
        }
        xp, mix_p, h_fin = _trunk_layer(xp, mod[l], dec_b, P, col, kinds, n_mix, None)
        bsz, seq = xp.shape[:2]
        new_k.append(mix_p[:, :, col["k"]:col["k"] + kv_w].reshape(bsz, seq, N_KV, HEAD_DIM))
        new_v.append(mix_p[:, :, col["v"]:col["v"] + kv_w].reshape(bsz, seq, N_KV, HEAD_DIM))
        new_h.append(h_fin)
        ctx = (cache_k2[:, l], cache_v2[:, l], state_lru[:, l], cos_t, sin_t)
        xs, _, _ = _trunk_layer(xs, mod[l], None, P, col, kinds, n_mix, ctx)
    return (xp, xs, jnp.stack(new_k, axis=1), jnp.stack(new_v, axis=1), jnp.stack(new_h, axis=1))
```

```python
import functools

import jax
import jax.numpy as jnp
from jax import lax
from jax.experimental import pallas as pl
from jax.experimental.pallas import tpu as pltpu

F32 = jnp.float32
BF16 = jnp.bfloat16

EPS = 1e-6
LANES = 128
SUBLANES = 8
POOL_WINDOWS = (2, 4, 8, 16)
N_HEADS = 8
N_KV = 2
HEAD_DIM = 128
GQA_G = N_HEADS // N_KV
WINDOW = 128
BLOCK = 128
GRID_W = 64
ROPE_BASE = 10000.0
LRU_BLOCKS = 4
CONV_W = 4
CONV_LEFT = 2
LRU_C = 8.0
N_BRANCH = 3
MOD_ROWS = 16

VMEM_LIMIT = 60 * 1024 * 1024


def _sigmoid(x):
    return 0.5 * (jnp.tanh(0.5 * x) + 1.0)


def _silu(x):
    return x * _sigmoid(x)


def _params(sem):
    return pltpu.CompilerParams(dimension_semantics=sem, vmem_limit_bytes=VMEM_LIMIT)


def _mod_kernel(c_ref, w_ref, b_ref, o_ref):
    s = _silu(c_ref[...]).astype(BF16)
    o_ref[...] = jnp.dot(s, w_ref[...].astype(BF16), preferred_element_type=F32) + b_ref[...]


def _modulation(cvec, w_ada, b_ada):
    depth, d, n = w_ada.shape
    tn = 768
    return pl.pallas_call(
        _mod_kernel,
        grid=(depth, n // tn),
        in_specs=[
            pl.BlockSpec((MOD_ROWS, d), lambda l, j: (0, 0)),
            pl.BlockSpec((None, d, tn), lambda l, j: (l, 0, j)),
            pl.BlockSpec((None, 1, tn), lambda l, j: (l, 0, j)),
        ],
        out_specs=pl.BlockSpec((None, MOD_ROWS, tn), lambda l, j: (l, 0, j)),
        out_shape=jax.ShapeDtypeStruct((depth, MOD_ROWS, n), F32),
        compiler_params=_params(("arbitrary", "arbitrary")),
        name="adaln_mod",
    )(cvec, w_ada, b_ada.reshape(depth, 1, n))


NORM_ROWS = 16
NORM_UNROLL = 8


def _prenorm_into(h_ref, x_ref, g_ref, mod_ref):
    gain = g_ref[...] * (1.0 + mod_ref[1])
    shift = mod_ref[0]

    def body(c, carry):
        rows = pl.ds(pl.multiple_of(c * NORM_ROWS, NORM_ROWS), NORM_ROWS)
        x = x_ref[rows, :]
        inv = lax.rsqrt(jnp.mean(x * x, axis=-1, keepdims=True) + EPS)
        h_ref[rows, :] = ((x * inv) * gain + shift).astype(BF16)
        return carry

    lax.fori_loop(0, x_ref.shape[0] // NORM_ROWS, body, 0, unroll=NORM_UNROLL)


def _inproj_kernel(x_ref, mod_ref, g_ref, w_ref, *rest, tn, kinds, rope):
    if rope:
        cos_ref, sin_ref, o_ref, h_ref = rest
        cos_k, sin_k = cos_ref[...], sin_ref[...]
        q_scale = HEAD_DIM ** -0.5 * LOG2E
        cos_q, sin_q = cos_k * q_scale, sin_k * q_scale
    else:
        o_ref, h_ref, ko_ref, vo_ref = rest
    tm = x_ref.shape[0]
    _prenorm_into(h_ref, x_ref, g_ref, mod_ref)
    slabs = tn // LANES
    for c in range(o_ref.shape[1] // tn):
        cols = slice(c * tn, (c + 1) * tn)
        res = jnp.dot(h_ref[...], w_ref[:, cols], preferred_element_type=F32)
        parts = []
        for s in range(slabs):
            part = res[:, s * LANES:(s + 1) * LANES]
            kind, head = kinds[c * slabs + s]
            if kind == "z":
                part = _silu(part)
            elif kind == "q" and rope:
                part = _rope(part, cos_q, sin_q)
            elif kind == "k" and rope:
                part = _rope(part, cos_k, sin_k)
            elif kind == "k":
                ko_ref[pl.ds(head, tm, stride=N_KV), :] = part
            elif kind == "v" and not rope:
                vo_ref[pl.ds(head, tm, stride=N_KV), :] = part
            parts.append(part)
        o_ref[:, cols] = jnp.concatenate(parts, axis=1)


def _in_projection(x2, mod, g_pre, w_in_b, layer, n_mix, seq, mod_row0, tm, tn, kinds, tables):
    rows, d = x2.shape
    if mod_row0 is None:
        mod_map = lambda i: ((i * tm) // seq, 0, 0, 0)
    else:
        mod_map = lambda i: (mod_row0, 0, 0, 0)
    in_specs = [
        pl.BlockSpec((tm, d), lambda i: (i, 0)),
        pl.BlockSpec((None, 3, 1, d), mod_map),
        pl.BlockSpec((1, d), lambda i: (0, 0)),
        pl.BlockSpec((None, d, n_mix), lambda i: (layer, 0, 0), pipeline_mode=pl.Buffered(1)),
    ]
    operands = [x2, mod, g_pre, w_in_b]
    out_specs = [pl.BlockSpec((tm, n_mix), lambda i: (i, 0)),
                 pl.BlockSpec((tm, d), lambda i: (i, 0))]
    out_shape = [jax.ShapeDtypeStruct((rows, n_mix), F32),
                 jax.ShapeDtypeStruct((rows, d), BF16)]
    if tables is not None:
        per_seq = seq // tm
        in_specs += [pl.BlockSpec((tm, HEAD_DIM), lambda i: (i % per_seq, 0))] * 2
        operands += list(tables)
    else:
        out_specs += [pl.BlockSpec((N_KV * tm, HEAD_DIM), lambda i: (i, 0))] * 2
        out_shape += [jax.ShapeDtypeStruct((N_KV * rows, HEAD_DIM), F32)] * 2
    return pl.pallas_call(
        functools.partial(_inproj_kernel, tn=tn, kinds=kinds, rope=tables is not None),
        grid=(rows // tm,),
        in_specs=in_specs,
        out_specs=out_specs,
        out_shape=out_shape,
        compiler_params=_params(("arbitrary",)),
        name="in_proj",
    )(*operands)


POOL_EDGE = 16


def _window_sum(ext_ref, pa_ref, pb_ref, lanes, base, w, tt):
    if w <= 4:
        acc = ext_ref[base:base + tt, lanes]
        for k in range(1, w):
            acc = acc + ext_ref[base + k:base + k + tt, lanes]
        return acc
    n = tt + w - 2
    pa_ref[0:n, :] = ext_ref[base:base + n, lanes] + ext_ref[base + 1:base + 1 + n, lanes]
    src, dst, s = pa_ref, pb_ref, 2
    while 2 * s < w:
        n = tt + w - 2 * s
        dst[0:n, :] = src[0:n, :] + src[s:s + n, :]
        src, dst, s = dst, src, 2 * s
    return src[0:tt, :] + src[s:s + tt, :]


def _pool_kernel(up_ref, uc_ref, un_ref, z_ref, wm_ref, sc_ref, o_ref, ext_ref, pa_ref, pb_ref,
                 d_ref, *, seq, tt):
    i = pl.program_id(1)
    nt = pl.num_programs(1)
    halo = SUBLANES
    ext_ref[0:halo, :] = jnp.where(i > 0, up_ref[...], 0.0)
    ext_ref[halo:halo + tt, :] = uc_ref[...]
    ext_ref[halo + tt:2 * halo + tt, :] = jnp.where(i < nt - 1, un_ref[...], 0.0)
    for g, w in enumerate(POOL_WINDOWS):
        lanes = slice(g * LANES, (g + 1) * LANES)
        acc = _window_sum(ext_ref, pa_ref, pb_ref, lanes, halo - w // 2, w, tt)
        d_ref[:, lanes] = (acc * (1.0 / w) - uc_ref[:, lanes]).astype(BF16)
        for r0 in (0, tt - POOL_EDGE):
            t = i * tt + r0 + lax.broadcasted_iota(jnp.int32, (POOL_EDGE, 1), 0)
            cnt = (jnp.minimum(t + w // 2, seq) - jnp.maximum(t - w // 2, 0)).astype(F32)
            edge = acc[r0:r0 + POOL_EDGE, :] / cnt - uc_ref[r0:r0 + POOL_EDGE, lanes]
            d_ref[r0:r0 + POOL_EDGE, lanes] = edge.astype(BF16)
        y = jnp.dot(d_ref[:, lanes], wm_ref[g], preferred_element_type=F32)
        y = (y * sc_ref[:, lanes]) * z_ref[:, lanes]
        o_ref[:, lanes] = y.astype(BF16)


def _pool_mixer(mix, w_map_b, pool_scale, tt):
    bsz, seq, _ = mix.shape
    pw = len(POOL_WINDOWS) * LANES
    nh = tt // SUBLANES
    last = seq // SUBLANES - 1
    return pl.pallas_call(
        functools.partial(_pool_kernel, seq=seq, tt=tt),
        grid=(bsz, seq // tt),
        in_specs=[
            pl.BlockSpec((None, SUBLANES, pw), lambda b, i: (b, jnp.maximum(i * nh - 1, 0), 0)),
            pl.BlockSpec((None, tt, pw), lambda b, i: (b, i, 0)),
            pl.BlockSpec((None, SUBLANES, pw), lambda b, i: (b, jnp.minimum((i + 1) * nh, last), 0)),
            pl.BlockSpec((None, tt, pw), lambda b, i: (b, i, 1)),
            pl.BlockSpec((len(POOL_WINDOWS), LANES, LANES), lambda b, i: (0, 0, 0)),
            pl.BlockSpec((1, pw), lambda b, i: (0, 0)),
        ],
        out_specs=pl.BlockSpec((None, tt, pw), lambda b, i: (b, i, 0)),
        out_shape=jax.ShapeDtypeStruct((bsz, seq, pw), BF16),
        scratch_shapes=[pltpu.VMEM((tt + 2 * SUBLANES, pw), F32),
                        pltpu.VMEM((tt + 2 * SUBLANES, LANES), F32),
                        pltpu.VMEM((tt + 2 * SUBLANES, LANES), F32),
                        pltpu.VMEM((tt, pw), BF16)],
        compiler_params=_params(("arbitrary", "arbitrary")),
        name="pool_mixer",
    )(mix, mix, mix, mix, w_map_b, pool_scale)


def _stack_heads(x):
    return jnp.concatenate([x[:, g * HEAD_DIM:(g + 1) * HEAD_DIM] for g in range(GQA_G)], axis=0)


def _unstack_heads(x, rows):
    return jnp.concatenate([x[g * rows:(g + 1) * rows, :] for g in range(GQA_G)], axis=1)


def _sink_column(sink_ref, kv, rows):
    head = lax.broadcasted_iota(jnp.int32, (GQA_G * rows, 1), 0) // rows
    col = jnp.full((GQA_G * rows, 1), sink_ref[kv * GQA_G], F32)
    for g in range(1, GQA_G):
        col = jnp.where(head == g, sink_ref[kv * GQA_G + g], col)
    return col


LOG2E = 1.4426950408889634

def _softmax_pv(s, sink, v):
    m = jnp.maximum(jnp.max(s, axis=-1, keepdims=True), sink)
    e = jnp.exp2(s - m)
    den = jnp.sum(e, axis=-1, keepdims=True) + jnp.exp2(sink - m)
    o = jnp.dot(e.astype(BF16), v, preferred_element_type=F32)
    return o / den


def _qk(q, k):
    return lax.dot_general(q, k, (((1,), (1,)), ((), ())), preferred_element_type=F32)


def _ctx_attn_kernel(sink_ref, q_ref, k_ref, v_ref, z_ref, o_ref, *, seq):
    kv = pl.program_id(1)
    scale = HEAD_DIM ** -0.5 * LOG2E
    q = _stack_heads(q_ref[...]).astype(BF16)
    s = _qk(q, k_ref[...].astype(BF16)) * scale
    o = _softmax_pv(s, _sink_column(sink_ref, kv, seq) * LOG2E, v_ref[...].astype(BF16))
    o_ref[...] = (_unstack_heads(o, seq) * z_ref[...]).astype(BF16)


def _context_attention(mix, sink, col):
    bsz, seq, _ = mix.shape
    gw = GQA_G * HEAD_DIM
    return pl.pallas_call(
        functools.partial(_ctx_attn_kernel, seq=seq),
        grid=(bsz, N_KV),
        in_specs=[
            pl.BlockSpec(memory_space=pltpu.SMEM),
            pl.BlockSpec((None, seq, gw), lambda b, kv: (b, 0, col["q"] // gw + kv)),
            pl.BlockSpec((None, seq, HEAD_DIM), lambda b, kv: (b, 0, col["k"] // HEAD_DIM + kv)),
            pl.BlockSpec((None, seq, HEAD_DIM), lambda b, kv: (b, 0, col["v"] // HEAD_DIM + kv)),
            pl.BlockSpec((None, seq, gw), lambda b, kv: (b, 0, col["z_attn"] // gw + kv)),
        ],
        out_specs=pl.BlockSpec((None, seq, gw), lambda b, kv: (b, 0, kv)),
        out_shape=jax.ShapeDtypeStruct((bsz, seq, N_KV * gw), BF16),
        compiler_params=_params(("arbitrary", "arbitrary")),
        name="ctx_attention",
    )(sink, mix, mix, mix, mix)


def _rope(x, cos, sin):
    lane = lax.broadcasted_iota(jnp.int32, x.shape, 1)
    quarter = HEAD_DIM // 4
    first = (lane % (2 * quarter)) < quarter
    partner = jnp.where(first, pltpu.roll(x, HEAD_DIM - quarter, 1), pltpu.roll(x, quarter, 1))
    return x * cos + partner * sin


def _lat_attn_kernel(sink_ref, q_ref, k_ref, v_ref, kx_ref, vx_ref, z_ref, o_ref,
                     kr_ref, vb_ref, bias_ref, *, seq, qc):
    kv = pl.program_id(1)
    ci = pl.program_id(2)
    nb = seq // BLOCK
    nb_c = qc // BLOCK
    span = 3 * BLOCK

    @pl.when(ci == 0)
    def _():
        zeros = jnp.zeros((BLOCK, HEAD_DIM), BF16)
        for ref in (kr_ref, vb_ref):
            ref[0:BLOCK, :] = zeros
            ref[BLOCK + seq:2 * BLOCK + seq, :] = zeros

        def fill(i, carry):
            rows = pl.ds(pl.multiple_of(i * BLOCK, BLOCK), BLOCK)
            dst = pl.ds(pl.multiple_of((i + 1) * BLOCK, BLOCK), BLOCK)
            kr_ref[dst, :] = k_ref[rows, :].astype(BF16)
            vb_ref[dst, :] = v_ref[rows, :].astype(BF16)
            return carry

        lax.fori_loop(0, nb, fill, 0, unroll=4)
        r = lax.broadcasted_iota(jnp.int32, (GQA_G * BLOCK, span), 0) % BLOCK
        c = lax.broadcasted_iota(jnp.int32, (GQA_G * BLOCK, span), 1)
        band = (c >= r) & (c <= r + 2 * WINDOW)
        neg = jnp.float32(-jnp.inf)
        bias_ref[0] = jnp.where(band & (c >= BLOCK), 0.0, neg)
        bias_ref[1] = jnp.where(band, 0.0, neg)
        bias_ref[2] = jnp.where(band & (c < 2 * BLOCK), 0.0, neg)

    sink = _sink_column(sink_ref, kv, BLOCK) * LOG2E
    kx = kx_ref[...].astype(BF16)
    vx = vx_ref[...].astype(BF16)

    def block(jb, carry):
        bi = ci * nb_c + jb
        rows = pl.ds(pl.multiple_of(jb * BLOCK, BLOCK), BLOCK)
        q = _stack_heads(q_ref[rows, :]).astype(BF16)
        win = pl.ds(pl.multiple_of(bi * BLOCK, BLOCK), span)
        variant = jnp.where(bi == 0, 0, jnp.where(bi == nb - 1, 2, 1))
        s = jnp.concatenate([_qk(q, kr_ref[win, :]) + bias_ref[variant], _qk(q, kx)], axis=1)
        m = jnp.maximum(jnp.max(s, axis=-1, keepdims=True), sink)
        e = jnp.exp2(s - m)
        den = jnp.sum(e, axis=-1, keepdims=True) + jnp.exp2(sink - m)
        e = e.astype(BF16)
        o = (jnp.dot(e[:, :span], vb_ref[win, :], preferred_element_type=F32)
             + jnp.dot(e[:, span:], vx, preferred_element_type=F32)) / den
        o_ref[rows, :] = (_unstack_heads(o, BLOCK) * z_ref[rows, :]).astype(BF16)
        return carry

    lax.fori_loop(0, nb_c, block, 0, unroll=min(4, nb_c))


def _latent_attention(mix, sink, k_ctx, v_ctx, col):
    bsz, seq, _ = mix.shape
    tc = k_ctx.shape[1]
    assert seq // BLOCK >= 2
    qc = _tile(seq, 1024)
    gw = GQA_G * HEAD_DIM
    kcol = col["k"] // HEAD_DIM
    vcol = col["v"] // HEAD_DIM
    return pl.pallas_call(
        functools.partial(_lat_attn_kernel, seq=seq, qc=qc),
        grid=(bsz, N_KV, seq // qc),
        in_specs=[
            pl.BlockSpec(memory_space=pltpu.SMEM),
            pl.BlockSpec((None, qc, gw), lambda b, kv, ci: (b, ci, col["q"] // gw + kv)),
            pl.BlockSpec((None, seq, HEAD_DIM), lambda b, kv, ci: (b, 0, kcol + kv)),
            pl.BlockSpec((None, seq, HEAD_DIM), lambda b, kv, ci: (b, 0, vcol + kv)),
            pl.BlockSpec((None, tc, HEAD_DIM), lambda b, kv, ci: (b, 0, kv)),
            pl.BlockSpec((None, tc, HEAD_DIM), lambda b, kv, ci: (b, 0, kv)),
            pl.BlockSpec((None, qc, gw), lambda b, kv, ci: (b, ci, col["z_attn"] // gw + kv)),
        ],
        out_specs=pl.BlockSpec((None, qc, gw), lambda b, kv, ci: (b, ci, kv)),
        out_shape=jax.ShapeDtypeStruct((bsz, seq, N_KV * gw), BF16),
        scratch_shapes=[
            pltpu.VMEM((seq + 2 * BLOCK, HEAD_DIM), BF16),
            pltpu.VMEM((seq + 2 * BLOCK, HEAD_DIM), BF16),
            pltpu.VMEM((3, GQA_G * BLOCK, 3 * BLOCK), F32),
        ],
        compiler_params=_params(("arbitrary", "arbitrary", "arbitrary")),
        name="latent_attention",
    )(sink, mix, mix, mix, k_ctx, v_ctx, mix)


def _rope_tables(seq):
    rows = seq // GRID_W
    row = jnp.repeat(jnp.arange(rows, dtype=F32), GRID_W)
    colp = jnp.tile(jnp.arange(GRID_W, dtype=F32), rows)
    n_freq = HEAD_DIM // 4
    inv = ROPE_BASE ** (-jnp.arange(n_freq, dtype=F32) / n_freq)
    ang_r = row[:, None] * inv[None]
    ang_c = colp[:, None] * inv[None]
    cos_t = jnp.concatenate([jnp.cos(ang_r)] * 2 + [jnp.cos(ang_c)] * 2, axis=-1)
    sin_t = jnp.concatenate([-jnp.sin(ang_r), jnp.sin(ang_r), -jnp.sin(ang_c), jnp.sin(ang_c)], axis=-1)
    return cos_t, sin_t


def _row_bcast(x, r):
    return jnp.broadcast_to(x[r:r + 1, :], x.shape)


SCAN_ROWS = SUBLANES * SUBLANES


def _segment_scan(a_ref, b_ref, t0, carry, reverse):
    idx = [pl.ds(t0 + k, SUBLANES, stride=SUBLANES) for k in range(SUBLANES)]
    a = [a_ref[i, :] for i in idx]
    b = [b_ref[i, :] for i in idx]
    order = list(range(SUBLANES - 1, -1, -1)) if reverse else list(range(SUBLANES))
    h = [None] * SUBLANES
    p = [None] * SUBLANES
    prev = None
    for k in order:
        h[k] = b[k] if prev is None else a[k] * h[prev] + b[k]
        p[k] = a[k] if prev is None else a[k] * p[prev]
        prev = k
    pp, hh = p[prev], h[prev]
    sub = lax.broadcasted_iota(jnp.int32, (SUBLANES, LANES), 0)
    for d in (1, 2, 4):
        keep = (sub < SUBLANES - d) if reverse else (sub >= d)
        shift = SUBLANES - d if reverse else d
        p_sh = jnp.where(keep, pltpu.roll(pp, shift, 0), 1.0)
        h_sh = jnp.where(keep, pltpu.roll(hh, shift, 0), 0.0)
        hh = pp * h_sh + hh
        pp = pp * p_sh
    end = pp * carry + hh
    if reverse:
        c_in = jnp.where(sub < SUBLANES - 1, pltpu.roll(end, SUBLANES - 1, 0), carry)
    else:
        c_in = jnp.where(sub >= 1, pltpu.roll(end, 1, 0), carry)
    for k in range(SUBLANES):
        b_ref[idx[k], :] = h[k] + p[k] * c_in
    return _row_bcast(end, 0 if reverse else SUBLANES - 1)


def _lru_kernel(u_ref, z_ref, cw_ref, cb_ref, gw_ref, gb_ref, lam_ref, h0_ref, y_ref, hfin_ref,
                up_ref, af_ref, bf_ref, ab_ref, bb_ref, *, seq, chunk):
    pad = SUBLANES
    up_ref[0:pad, :] = jnp.zeros((pad, LANES), F32)
    up_ref[pad:pad + seq, :] = u_ref[...]
    up_ref[pad + seq:2 * pad + seq, :] = jnp.zeros((pad, LANES), F32)
    nlam = -lam_ref[...]
    softplus = jnp.maximum(nlam, 0.0) + jnp.log1p(jnp.exp(-jnp.abs(nlam)))
    half_rate = (-0.5 * LRU_C) * softplus

    def gates(ci, carry):
        rows = pl.ds(pl.multiple_of(ci * chunk, chunk), chunk)
        xc = cb_ref[...]
        for j in range(CONV_W):
            tap = up_ref[pl.ds(ci * chunk + (pad - CONV_LEFT + j), chunk), :]
            xc = xc + tap * cw_ref[j:j + 1, :]
        g = jnp.dot(xc.astype(BF16), gw_ref[...], preferred_element_type=F32) + gb_ref[...]
        xc_half = 0.5 * xc
        for d, (a_ref, b_ref) in enumerate(((af_ref, bf_ref), (ab_ref, bb_ref))):
            t_r = jnp.tanh(g[:, (2 * d) * LANES:(2 * d + 1) * LANES])
            t_i = jnp.tanh(g[:, (2 * d + 1) * LANES:(2 * d + 2) * LANES])
            log_a = (t_r + 1.0) * half_rate[d:d + 1, :]
            a = jnp.exp(log_a)
            one_minus_a2 = -jnp.tanh(log_a) * (a * a + 1.0)
            a_ref[rows, :] = a
            b_ref[rows, :] = jnp.sqrt(jnp.maximum(one_minus_a2, 0.0)) * ((t_i + 1.0) * xc_half)
        return carry

    lax.fori_loop(0, seq // chunk, gates, 0, unroll=min(4, seq // chunk))

    n_scan = seq // SCAN_ROWS

    def scan(ci, carry):
        c_f, c_b = carry
        c_f = _segment_scan(af_ref, bf_ref, ci * SCAN_ROWS, c_f, False)
        c_b = _segment_scan(ab_ref, bb_ref, (n_scan - 1 - ci) * SCAN_ROWS, c_b, True)
        return c_f, c_b

    h0 = h0_ref[...]
    c_f, c_b = lax.fori_loop(0, n_scan, scan, (jnp.broadcast_to(h0[0:1, :], (SUBLANES, LANES)),
                                               jnp.broadcast_to(h0[1:2, :], (SUBLANES, LANES))),
                             unroll=2)
    hfin_ref[0:1, :] = c_f[0:1, :]
    hfin_ref[1:2, :] = c_b[0:1, :]

    def gate_out(ci, carry):
        rows = pl.ds(pl.multiple_of(ci * chunk, chunk), chunk)
        y_ref[rows, :] = ((bf_ref[rows, :] + bb_ref[rows, :]) * z_ref[rows, :]).astype(BF16)
        return carry

    lax.fori_loop(0, seq // chunk, gate_out, 0, unroll=2)


def _rglru(mix, conv_w, conv_b, gate_w_b, gate_b, lam, h0, col, chunk):
    bsz, seq, _ = mix.shape
    assert seq % SCAN_ROWS == 0 and seq % chunk == 0
    lw = LRU_BLOCKS * LANES
    ucol = col["u_lru"] // LANES
    zcol = col["z_lru"] // LANES
    return pl.pallas_call(
        functools.partial(_lru_kernel, seq=seq, chunk=chunk),
        grid=(bsz, LRU_BLOCKS),
        in_specs=[
            pl.BlockSpec((None, seq, LANES), lambda b, n: (b, 0, ucol + n)),
            pl.BlockSpec((None, seq, LANES), lambda b, n: (b, 0, zcol + n)),
            pl.BlockSpec((CONV_W, LANES), lambda b, n: (0, n)),
            pl.BlockSpec((1, LANES), lambda b, n: (0, n)),
            pl.BlockSpec((None, LANES, 4 * LANES), lambda b, n: (n, 0, 0)),
            pl.BlockSpec((None, 1, 4 * LANES), lambda b, n: (n, 0, 0)),
            pl.BlockSpec((2, LANES), lambda b, n: (0, n)),
            pl.BlockSpec((None, 2, LANES), lambda b, n: (b, 0, n)),
        ],
        out_specs=[
            pl.BlockSpec((None, seq, LANES), lambda b, n: (b, 0, n)),
            pl.BlockSpec((None, 2, LANES), lambda b, n: (b, 0, n)),
        ],
        out_shape=[
            jax.ShapeDtypeStruct((bsz, seq, lw), BF16),
            jax.ShapeDtypeStruct((bsz, 2, lw), F32),
        ],
        scratch_shapes=[pltpu.VMEM((seq + 2 * SUBLANES, LANES), F32)]
        + [pltpu.VMEM((seq, LANES), F32)] * 4,
        compiler_params=_params(("arbitrary", "arbitrary")),
        name="rglru",
    )(mix, mix, conv_w, conv_b, gate_w_b, gate_b, lam, h0)


def _out_kernel(x_ref, h_ref, mod_ref, gpost_ref, yp_ref, ya_ref, yl_ref,
                wm0_ref, wm1_ref, wm2_ref, wp_ref, wa_ref, wl_ref, wo_ref, o_ref, mg_ref, acc_ref,
                *, n_merge, n_out, tn, tn_out):
    j = pl.program_id(1)

    @pl.when(j < n_merge)
    def _():
        h = h_ref[...]
        merged = None
        for wm_ref, y_ref, wb_ref in ((wm0_ref, yp_ref, wp_ref), (wm1_ref, ya_ref, wa_ref),
                                      (wm2_ref, yl_ref, wl_ref)):
            gate = _sigmoid(jnp.dot(h, wm_ref[...], preferred_element_type=F32))
            term = gate * jnp.dot(y_ref[...], wb_ref[...], preferred_element_type=F32)
            merged = term if merged is None else merged + term
        merged = merged.astype(BF16)
        for c in range(n_merge):
            @pl.when(j == c)
            def _():
                mg_ref[:, c * tn:(c + 1) * tn] = merged

    @pl.when(j == n_merge)
    def _():
        for c in range(n_out):
            cols = slice(c * tn_out, (c + 1) * tn_out)
            acc_ref[:, cols] = jnp.dot(mg_ref[...], wo_ref[:, cols], preferred_element_type=F32)
        gain = gpost_ref[...] * mod_ref[2]

        def body(c, carry):
            rows = pl.ds(pl.multiple_of(c * NORM_ROWS, NORM_ROWS), NORM_ROWS)
            out = acc_ref[rows, :]
            inv = lax.rsqrt(jnp.mean(out * out, axis=-1, keepdims=True) + EPS)
            o_ref[rows, :] = x_ref[rows, :] + (out * inv) * gain
            return carry

        lax.fori_loop(0, o_ref.shape[0] // NORM_ROWS, body, 0, unroll=NORM_UNROLL)


def _out_stage(x2, h2, mod, g_post, y_pool, y_attn, y_lru, w_in_b, w_pool_o, w_attn_o, w_lru_o,
               w_out, layer, n_mix, seq, mod_row0, tm, tn, tn_out):
    rows, d = x2.shape
    n_merge = d // tn
    n_out = d // tn_out
    if mod_row0 is None:
        mod_map = lambda i, j: ((i * tm) // seq, 0, 0, 0)
    else:
        mod_map = lambda i, j: (mod_row0, 0, 0, 0)
    mstep = lambda j: jnp.minimum(j, n_merge - 1)
    mcol = lambda br: (lambda i, j: (layer, 0, (n_mix + br * d) // tn + mstep(j)))
    bcol = lambda i, j: (layer, 0, mstep(j))
    return pl.pallas_call(
        functools.partial(_out_kernel, n_merge=n_merge, n_out=n_out, tn=tn, tn_out=tn_out),
        grid=(rows // tm, n_merge + 1),
        in_specs=[
            pl.BlockSpec((tm, d), lambda i, j: (i, 0)),
            pl.BlockSpec((tm, d), lambda i, j: (i, 0)),
            pl.BlockSpec((None, 3, 1, d), mod_map),
            pl.BlockSpec((1, d), lambda i, j: (0, 0)),
            pl.BlockSpec((tm, y_pool.shape[1]), lambda i, j: (i, 0)),
            pl.BlockSpec((tm, y_attn.shape[1]), lambda i, j: (i, 0)),
            pl.BlockSpec((tm, y_lru.shape[1]), lambda i, j: (i, 0)),
            pl.BlockSpec((None, d, tn), mcol(0)),
            pl.BlockSpec((None, d, tn), mcol(1)),
            pl.BlockSpec((None, d, tn), mcol(2)),
            pl.BlockSpec((None, w_pool_o.shape[1], tn), bcol),
            pl.BlockSpec((None, w_attn_o.shape[1], tn), bcol),
            pl.BlockSpec((None, w_lru_o.shape[1], tn), bcol),
            pl.BlockSpec((None, d, d), lambda i, j: (layer, 0, 0), pipeline_mode=pl.Buffered(1)),
        ],
        out_specs=pl.BlockSpec((tm, d), lambda i, j: (i, 0)),
        out_shape=jax.ShapeDtypeStruct((rows, d), F32),
        scratch_shapes=[pltpu.VMEM((tm, d), BF16), pltpu.VMEM((tm, d), F32)],
        compiler_params=_params(("arbitrary", "arbitrary")),
        name="out_stage",
    )(x2, h2, mod, g_post, y_pool, y_attn, y_lru, w_in_b, w_in_b, w_in_b,
      w_pool_o, w_attn_o, w_lru_o, w_out)


def _tile(n, want):
    t = min(n, want)
    assert n % t == 0, (n, t)
    return t


def _trunk_layer(x, mod, mod_row0, P, col, kinds, n_mix, ctx):
    bsz, seq, d = x.shape
    x2 = x.reshape(bsz * seq, d)
    rows = bsz * seq
    tm_in = _tile(rows if mod_row0 is not None else seq, 512)
    tables = None if ctx is None else ctx[3:]
    mix, h2, *new_kv = _in_projection(x2, mod, P["g_pre"], P["w_in"], P["layer"], n_mix, seq,
                                      mod_row0, tm_in, 768, kinds, tables)
    mix = mix.reshape(bsz, seq, n_mix)
    y_pool = _pool_mixer(mix, P["w_pool_map"], P["pool_scale"], _tile(seq, 512))
    if ctx is None:
        y_attn = _context_attention(mix, P["attn_sink"], col)
        h0 = jnp.zeros((bsz, 2, LRU_BLOCKS * LANES), F32)
    else:
        k_ctx, v_ctx, h0 = ctx[:3]
        y_attn = _latent_attention(mix, P["attn_sink"], k_ctx, v_ctx, col)
    y_lru, h_fin = _rglru(mix, P["lru_conv_w"], P["lru_conv_b"], P["lru_gate_w"], P["lru_gate_b"],
                          P["lru_lambda"], h0, col, _tile(seq, 128))
    tm_out = _tile(rows if mod_row0 is not None else seq, 512)
    x_new = _out_stage(x2, h2, mod, P["g_post"], y_pool.reshape(rows, -1),
                       y_attn.reshape(rows, -1), y_lru.reshape(rows, -1), P["w_in"], P["w_pool_o"],
                       P["w_attn_o"], P["w_lru_o"], P["w_out"], P["layer"], n_mix, seq, mod_row0,
                       tm_out, 512, 512)
    new_kv = [a.reshape(bsz, seq, N_KV, HEAD_DIM) for a in new_kv]
    return x_new.reshape(bsz, seq, d), new_kv, h_fin


def kernel(x_prompt, x_sample, cache_k, cache_v, state_lru, c, c_ctx, g_pre, g_post, w_ada, b_ada,
           w_in, w_pool_map, pool_scale, attn_sink, lru_conv_w, lru_conv_b, lru_gate_w, lru_gate_b,
           lru_lambda, w_pool_o, w_attn_o, w_lru_o, w_out):
    depth = w_in.shape[0]
    d = x_prompt.shape[-1]
    dec_b, dec_seq, _ = x_sample.shape
    pool_w = len(POOL_WINDOWS) * LANES
    attn_w = N_HEADS * HEAD_DIM
    kv_w = N_KV * HEAD_DIM
    lru_w = LRU_BLOCKS * LANES
    widths = (("u_pool", pool_w), ("z_pool", pool_w), ("q", attn_w), ("k", kv_w), ("v", kv_w),
              ("z_attn", attn_w), ("u_lru", lru_w), ("z_lru", lru_w))
    col, n_mix, kinds = {}, 0, ()
    for name, w in widths:
        col[name] = n_mix
        n_mix += w
        kind = name[0] if name[0] in "zqkv" else ""
        kinds += tuple((kind, s) for s in range(w // LANES))
    assert w_in.shape[-1] == n_mix + N_BRANCH * d
    assert dec_b < MOD_ROWS

    cvec = jnp.zeros((MOD_ROWS, d), F32).at[:dec_b].set(c).at[dec_b].set(c_ctx)
    mod = _modulation(cvec, w_ada, b_ada)
    mod = mod.reshape(depth, MOD_ROWS, 3, 1, d)
    cos_t, sin_t = _rope_tables(dec_seq)
    past = cache_k.shape[2]
    cache_k2 = cache_k.reshape(dec_b, depth, past, kv_w)
    cache_v2 = cache_v.reshape(dec_b, depth, past, kv_w)

    w_in_b, w_out_b = w_in.astype(BF16), w_out.astype(BF16)
    w_pool_o_b, w_attn_o_b, w_lru_o_b = (w.astype(BF16) for w in (w_pool_o, w_attn_o, w_lru_o))

    xp, xs = x_prompt, x_sample
    new_k, new_v, new_h = [], [], []
    for l in range(depth):
        P = {
            "layer": l,
            "g_pre": g_pre[l][None], "g_post": g_post[l][None],
            "w_in": w_in_b,
            "w_pool_map": w_pool_map[l].astype(BF16), "pool_scale": pool_scale[l][None],
            "attn_sink": attn_sink[l],
            "lru_conv_w": lru_conv_w[l], "lru_conv_b": lru_conv_b[l][None],
            "lru_gate_w": (0.5 * lru_gate_w[l]).transpose(2, 3, 0, 1, 4).reshape(
                LRU_BLOCKS, LANES, 4 * LANES).astype(BF16),
            "lru_gate_b": (0.5 * lru_gate_b[l]).reshape(2, 2, LRU_BLOCKS, LANES).transpose(
                2, 0, 1, 3).reshape(LRU_BLOCKS, 1, 4 * LANES),
            "lru_lambda": lru_lambda[l],
            "w_pool_o": w_pool_o_b, "w_attn_o": w_attn_o_b, "w_lru_o": w_lru_o_b, "w_out": w_out_b,
        }
        xp, (k_l, v_l), h_fin = _trunk_layer(xp, mod[l], dec_b, P, col, kinds, n_mix, None)
        new_k.append(k_l)
        new_v.append(v_l)
        new_h.append(h_fin)
        ctx = (cache_k2[:, l], cache_v2[:, l], state_lru[:, l], cos_t, sin_t)
        xs, _, _ = _trunk_layer(xs, mod[l], None, P, col, kinds, n_mix, ctx)
    return (xp, xs, jnp.stack(new_k, axis=1), jnp.stack(new_v, axis=1), jnp.stack(new_h, axis=1))
```

```python
import functools

import jax
import jax.numpy as jnp
from jax import lax
from jax.experimental import pallas as pl
from jax.experimental.pallas import tpu as pltpu

F32 = jnp.float32
BF16 = jnp.bfloat16

EPS = 1e-6
LANES = 128
SUBLANES = 8
POOL_WINDOWS = (2, 4, 8, 16)
N_HEADS = 8
N_KV = 2
HEAD_DIM = 128
GQA_G = N_HEADS // N_KV
WINDOW = 128
BLOCK = 128
GRID_W = 64
ROPE_BASE = 10000.0
LRU_BLOCKS = 4
CONV_W = 4
CONV_LEFT = 2
LRU_C = 8.0
N_BRANCH = 3
MOD_ROWS = 16

VMEM_LIMIT = 60 * 1024 * 1024


def _sigmoid(x):
    return 0.5 * (jnp.tanh(0.5 * x) + 1.0)


def _silu(x):
    return x * _sigmoid(x)


def _params(sem):
    return pltpu.CompilerParams(dimension_semantics=sem, vmem_limit_bytes=VMEM_LIMIT)


def _mod_kernel(c_ref, w_ref, b_ref, o_ref):
    s = _silu(c_ref[...]).astype(BF16)
    o_ref[...] = jnp.dot(s, w_ref[...].astype(BF16), preferred_element_type=F32) + b_ref[...]


def _modulation(cvec, w_ada, b_ada):
    depth, d, n = w_ada.shape
    tn = 768
    return pl.pallas_call(
        _mod_kernel,
        grid=(depth, n // tn),
        in_specs=[
            pl.BlockSpec((MOD_ROWS, d), lambda l, j: (0, 0)),
            pl.BlockSpec((None, d, tn), lambda l, j: (l, 0, j)),
            pl.BlockSpec((None, 1, tn), lambda l, j: (l, 0, j)),
        ],
        out_specs=pl.BlockSpec((None, MOD_ROWS, tn), lambda l, j: (l, 0, j)),
        out_shape=jax.ShapeDtypeStruct((depth, MOD_ROWS, n), F32),
        compiler_params=_params(("arbitrary", "arbitrary")),
        name="adaln_mod",
    )(cvec, w_ada, b_ada.reshape(depth, 1, n))


NORM_ROWS = 16
NORM_UNROLL = 8


def _prenorm_into(h_ref, x_ref, g_ref, mod_ref):
    gain = g_ref[...] * (1.0 + mod_ref[1])
    shift = mod_ref[0]

    def body(c, carry):
        rows = pl.ds(pl.multiple_of(c * NORM_ROWS, NORM_ROWS), NORM_ROWS)
        x = x_ref[rows, :]
        inv = lax.rsqrt(jnp.mean(x * x, axis=-1, keepdims=True) + EPS)
        h_ref[rows, :] = ((x * inv) * gain + shift).astype(BF16)
        return carry

    lax.fori_loop(0, x_ref.shape[0] // NORM_ROWS, body, 0, unroll=NORM_UNROLL)


def _inproj_kernel(x_ref, mod_ref, g_ref, w_ref, *rest, tn, kinds, rope):
    if rope:
        cos_ref, sin_ref, o_ref, h_ref = rest
        cos_k, sin_k = cos_ref[...], sin_ref[...]
        q_scale = HEAD_DIM ** -0.5 * LOG2E
        cos_q, sin_q = cos_k * q_scale, sin_k * q_scale
    else:
        o_ref, h_ref, ko_ref, vo_ref = rest
    tm = x_ref.shape[0]
    _prenorm_into(h_ref, x_ref, g_ref, mod_ref)
    slabs = tn // LANES
    for c in range(o_ref.shape[1] // tn):
        cols = slice(c * tn, (c + 1) * tn)
        res = jnp.dot(h_ref[...], w_ref[:, cols], preferred_element_type=F32)
        parts = []
        for s in range(slabs):
            part = res[:, s * LANES:(s + 1) * LANES]
            kind, head = kinds[c * slabs + s]
            if kind == "z":
                part = _silu(part)
            elif kind == "q" and rope:
                part = _rope(part, cos_q, sin_q)
            elif kind == "k" and rope:
                part = _rope(part, cos_k, sin_k)
            elif kind == "k":
                ko_ref[pl.ds(head, tm, stride=N_KV), :] = part
            elif kind == "v" and not rope:
                vo_ref[pl.ds(head, tm, stride=N_KV), :] = part
            parts.append(part)
        o_ref[:, cols] = jnp.concatenate(parts, axis=1)


def _in_projection(x2, mod, g_pre, w_in_b, layer, n_mix, seq, mod_row0, tm, tn, kinds, tables):
    rows, d = x2.shape
    if mod_row0 is None:
        mod_map = lambda i: ((i * tm) // seq, 0, 0, 0)
    else:
        mod_map = lambda i: (mod_row0, 0, 0, 0)
    in_specs = [
        pl.BlockSpec((tm, d), lambda i: (i, 0)),
        pl.BlockSpec((None, 3, 1, d), mod_map),
        pl.BlockSpec((1, d), lambda i: (0, 0)),
        pl.BlockSpec((None, d, n_mix), lambda i: (layer, 0, 0), pipeline_mode=pl.Buffered(1)),
    ]
    operands = [x2, mod, g_pre, w_in_b]
    out_specs = [pl.BlockSpec((tm, n_mix), lambda i: (i, 0)),
                 pl.BlockSpec((tm, d), lambda i: (i, 0))]
    out_shape = [jax.ShapeDtypeStruct((rows, n_mix), F32),
                 jax.ShapeDtypeStruct((rows, d), BF16)]
    if tables is not None:
        per_seq = seq // tm
        in_specs += [pl.BlockSpec((tm, HEAD_DIM), lambda i: (i % per_seq, 0))] * 2
        operands += list(tables)
    else:
        out_specs += [pl.BlockSpec((N_KV * tm, HEAD_DIM), lambda i: (i, 0))] * 2
        out_shape += [jax.ShapeDtypeStruct((N_KV * rows, HEAD_DIM), F32)] * 2
    return pl.pallas_call(
        functools.partial(_inproj_kernel, tn=tn, kinds=kinds, rope=tables is not None),
        grid=(rows // tm,),
        in_specs=in_specs,
        out_specs=out_specs,
        out_shape=out_shape,
        compiler_params=_params(("arbitrary",)),
        name="in_proj",
    )(*operands)


POOL_EDGE = 16


def _window_sum(ext_ref, pa_ref, pb_ref, lanes, base, w, tt):
    if w <= 4:
        acc = ext_ref[base:base + tt, lanes]
        for k in range(1, w):
            acc = acc + ext_ref[base + k:base + k + tt, lanes]
        return acc
    n = tt + w - 2
    pa_ref[0:n, :] = ext_ref[base:base + n, lanes] + ext_ref[base + 1:base + 1 + n, lanes]
    src, dst, s = pa_ref, pb_ref, 2
    while 2 * s < w:
        n = tt + w - 2 * s
        dst[0:n, :] = src[0:n, :] + src[s:s + n, :]
        src, dst, s = dst, src, 2 * s
    return src[0:tt, :] + src[s:s + tt, :]


def _pool_kernel(up_ref, uz_ref, un_ref, wm_ref, sc_ref, o_ref, ext_ref, pa_ref, pb_ref,
                 d_ref, *, seq, tt):
    i = pl.program_id(1)
    nt = pl.num_programs(1)
    halo = SUBLANES
    pw = o_ref.shape[1]
    uc_ref = uz_ref.at[:, 0:pw]
    z_ref = uz_ref.at[:, pw:2 * pw]
    ext_ref[0:halo, :] = jnp.where(i > 0, up_ref[...], 0.0)
    ext_ref[halo:halo + tt, :] = uc_ref[...]
    ext_ref[halo + tt:2 * halo + tt, :] = jnp.where(i < nt - 1, un_ref[...], 0.0)
    for g, w in enumerate(POOL_WINDOWS):
        lanes = slice(g * LANES, (g + 1) * LANES)
        acc = _window_sum(ext_ref, pa_ref, pb_ref, lanes, halo - w // 2, w, tt)
        d_ref[:, lanes] = (acc * (1.0 / w) - uc_ref[:, lanes]).astype(BF16)
        for r0 in (0, tt - POOL_EDGE):
            t = i * tt + r0 + lax.broadcasted_iota(jnp.int32, (POOL_EDGE, 1), 0)
            cnt = (jnp.minimum(t + w // 2, seq) - jnp.maximum(t - w // 2, 0)).astype(F32)
            edge = acc[r0:r0 + POOL_EDGE, :] / cnt - uc_ref[r0:r0 + POOL_EDGE, lanes]
            d_ref[r0:r0 + POOL_EDGE, lanes] = edge.astype(BF16)
        y = jnp.dot(d_ref[:, lanes], wm_ref[g], preferred_element_type=F32)
        y = (y * sc_ref[:, lanes]) * z_ref[:, lanes]
        o_ref[:, lanes] = y.astype(BF16)


def _pool_mixer(mix, w_map_b, pool_scale, tt):
    bsz, seq, _ = mix.shape
    pw = len(POOL_WINDOWS) * LANES
    nh = tt // SUBLANES
    last = seq // SUBLANES - 1
    return pl.pallas_call(
        functools.partial(_pool_kernel, seq=seq, tt=tt),
        grid=(bsz, seq // tt),
        in_specs=[
            pl.BlockSpec((None, SUBLANES, pw), lambda b, i: (b, jnp.maximum(i * nh - 1, 0), 0)),
            pl.BlockSpec((None, tt, 2 * pw), lambda b, i: (b, i, 0)),
            pl.BlockSpec((None, SUBLANES, pw), lambda b, i: (b, jnp.minimum((i + 1) * nh, last), 0)),
            pl.BlockSpec((len(POOL_WINDOWS), LANES, LANES), lambda b, i: (0, 0, 0)),
            pl.BlockSpec((1, pw), lambda b, i: (0, 0)),
        ],
        out_specs=pl.BlockSpec((None, tt, pw), lambda b, i: (b, i, 0)),
        out_shape=jax.ShapeDtypeStruct((bsz, seq, pw), BF16),
        scratch_shapes=[pltpu.VMEM((tt + 2 * SUBLANES, pw), F32),
                        pltpu.VMEM((tt + 2 * SUBLANES, LANES), F32),
                        pltpu.VMEM((tt + 2 * SUBLANES, LANES), F32),
                        pltpu.VMEM((tt, pw), BF16)],
        compiler_params=_params(("arbitrary", "arbitrary")),
        name="pool_mixer",
    )(mix, mix, mix, w_map_b, pool_scale)


def _stack_heads(x):
    return jnp.concatenate([x[:, g * HEAD_DIM:(g + 1) * HEAD_DIM] for g in range(GQA_G)], axis=0)


def _unstack_heads(x, rows):
    return jnp.concatenate([x[g * rows:(g + 1) * rows, :] for g in range(GQA_G)], axis=1)


def _sink_column(sink_ref, kv, rows):
    head = lax.broadcasted_iota(jnp.int32, (GQA_G * rows, 1), 0) // rows
    col = jnp.full((GQA_G * rows, 1), sink_ref[kv * GQA_G], F32)
    for g in range(1, GQA_G):
        col = jnp.where(head == g, sink_ref[kv * GQA_G + g], col)
    return col


LOG2E = 1.4426950408889634

def _softmax_pv(s, sink, v):
    m = jnp.maximum(jnp.max(s, axis=-1, keepdims=True), sink)
    e = jnp.exp2(s - m)
    den = jnp.sum(e, axis=-1, keepdims=True) + jnp.exp2(sink - m)
    o = jnp.dot(e.astype(BF16), v, preferred_element_type=F32)
    return o / den


def _qk(q, k):
    return lax.dot_general(q, k, (((1,), (1,)), ((), ())), preferred_element_type=F32)


def _ctx_attn_kernel(sink_ref, q_ref, k_ref, v_ref, z_ref, o_ref, *, seq):
    kv = pl.program_id(1)
    scale = HEAD_DIM ** -0.5 * LOG2E
    q = _stack_heads(q_ref[...]).astype(BF16)
    s = _qk(q, k_ref[...].astype(BF16)) * scale
    o = _softmax_pv(s, _sink_column(sink_ref, kv, seq) * LOG2E, v_ref[...].astype(BF16))
    o_ref[...] = (_unstack_heads(o, seq) * z_ref[...]).astype(BF16)


def _context_attention(mix, sink, col):
    bsz, seq, _ = mix.shape
    gw = GQA_G * HEAD_DIM
    return pl.pallas_call(
        functools.partial(_ctx_attn_kernel, seq=seq),
        grid=(bsz, N_KV),
        in_specs=[
            pl.BlockSpec(memory_space=pltpu.SMEM),
            pl.BlockSpec((None, seq, gw), lambda b, kv: (b, 0, col["q"] // gw + kv)),
            pl.BlockSpec((None, seq, HEAD_DIM), lambda b, kv: (b, 0, col["k"] // HEAD_DIM + kv)),
            pl.BlockSpec((None, seq, HEAD_DIM), lambda b, kv: (b, 0, col["v"] // HEAD_DIM + kv)),
            pl.BlockSpec((None, seq, gw), lambda b, kv: (b, 0, col["z_attn"] // gw + kv)),
        ],
        out_specs=pl.BlockSpec((None, seq, gw), lambda b, kv: (b, 0, kv)),
        out_shape=jax.ShapeDtypeStruct((bsz, seq, N_KV * gw), BF16),
        compiler_params=_params(("arbitrary", "arbitrary")),
        name="ctx_attention",
    )(sink, mix, mix, mix, mix)


def _rope(x, cos, sin):
    lane = lax.broadcasted_iota(jnp.int32, x.shape, 1)
    quarter = HEAD_DIM // 4
    first = (lane % (2 * quarter)) < quarter
    partner = jnp.where(first, pltpu.roll(x, HEAD_DIM - quarter, 1), pltpu.roll(x, quarter, 1))
    return x * cos + partner * sin


def _lat_attn_kernel(sink_ref, q_ref, k_ref, v_ref, kx_ref, vx_ref, z_ref, o_ref,
                     kr_ref, vb_ref, bias_ref, *, seq, qc):
    kv = pl.program_id(1)
    ci = pl.program_id(2)
    nb = seq // BLOCK
    nb_c = qc // BLOCK
    span = 3 * BLOCK

    @pl.when(ci == 0)
    def _():
        zeros = jnp.zeros((BLOCK, HEAD_DIM), BF16)
        for ref in (kr_ref, vb_ref):
            ref[0:BLOCK, :] = zeros
            ref[BLOCK + seq:2 * BLOCK + seq, :] = zeros

        def fill(i, carry):
            rows = pl.ds(pl.multiple_of(i * BLOCK, BLOCK), BLOCK)
            dst = pl.ds(pl.multiple_of((i + 1) * BLOCK, BLOCK), BLOCK)
            kr_ref[dst, :] = k_ref[rows, :].astype(BF16)
            vb_ref[dst, :] = v_ref[rows, :].astype(BF16)
            return carry

        lax.fori_loop(0, nb, fill, 0, unroll=4)
        r = lax.broadcasted_iota(jnp.int32, (GQA_G * BLOCK, span), 0) % BLOCK
        c = lax.broadcasted_iota(jnp.int32, (GQA_G * BLOCK, span), 1)
        band = (c >= r) & (c <= r + 2 * WINDOW)
        neg = jnp.float32(-jnp.inf)
        bias_ref[0] = jnp.where(band & (c >= BLOCK), 0.0, neg)
        bias_ref[1] = jnp.where(band, 0.0, neg)
        bias_ref[2] = jnp.where(band & (c < 2 * BLOCK), 0.0, neg)

    sink = _sink_column(sink_ref, kv, BLOCK) * LOG2E
    kx = kx_ref[...].astype(BF16)
    vx = vx_ref[...].astype(BF16)

    def block(jb, carry):
        bi = ci * nb_c + jb
        rows = pl.ds(pl.multiple_of(jb * BLOCK, BLOCK), BLOCK)
        q = _stack_heads(q_ref[rows, :]).astype(BF16)
        win = pl.ds(pl.multiple_of(bi * BLOCK, BLOCK), span)
        variant = jnp.where(bi == 0, 0, jnp.where(bi == nb - 1, 2, 1))
        s = jnp.concatenate([_qk(q, kr_ref[win, :]) + bias_ref[variant], _qk(q, kx)], axis=1)
        m = jnp.maximum(jnp.max(s, axis=-1, keepdims=True), sink)
        e = jnp.exp2(s - m)
        den = jnp.sum(e, axis=-1, keepdims=True) + jnp.exp2(sink - m)
        e = e.astype(BF16)
        o = (jnp.dot(e[:, :span], vb_ref[win, :], preferred_element_type=F32)
             + jnp.dot(e[:, span:], vx, preferred_element_type=F32)) / den
        o_ref[rows, :] = (_unstack_heads(o, BLOCK) * z_ref[rows, :]).astype(BF16)
        return carry

    lax.fori_loop(0, nb_c, block, 0, unroll=min(4, nb_c))


def _latent_attention(mix, sink, k_ctx, v_ctx, col):
    bsz, seq, _ = mix.shape
    tc = k_ctx.shape[1]
    assert seq // BLOCK >= 2
    qc = _tile(seq, 1024)
    gw = GQA_G * HEAD_DIM
    kcol = col["k"] // HEAD_DIM
    vcol = col["v"] // HEAD_DIM
    return pl.pallas_call(
        functools.partial(_lat_attn_kernel, seq=seq, qc=qc),
        grid=(bsz, N_KV, seq // qc),
        in_specs=[
            pl.BlockSpec(memory_space=pltpu.SMEM),
            pl.BlockSpec((None, qc, gw), lambda b, kv, ci: (b, ci, col["q"] // gw + kv)),
            pl.BlockSpec((None, seq, HEAD_DIM), lambda b, kv, ci: (b, 0, kcol + kv)),
            pl.BlockSpec((None, seq, HEAD_DIM), lambda b, kv, ci: (b, 0, vcol + kv)),
            pl.BlockSpec((None, tc, HEAD_DIM), lambda b, kv, ci: (b, 0, kv)),
            pl.BlockSpec((None, tc, HEAD_DIM), lambda b, kv, ci: (b, 0, kv)),
            pl.BlockSpec((None, qc, gw), lambda b, kv, ci: (b, ci, col["z_attn"] // gw + kv)),
        ],
        out_specs=pl.BlockSpec((None, qc, gw), lambda b, kv, ci: (b, ci, kv)),
        out_shape=jax.ShapeDtypeStruct((bsz, seq, N_KV * gw), BF16),
        scratch_shapes=[
            pltpu.VMEM((seq + 2 * BLOCK, HEAD_DIM), BF16),
            pltpu.VMEM((seq + 2 * BLOCK, HEAD_DIM), BF16),
            pltpu.VMEM((3, GQA_G * BLOCK, 3 * BLOCK), F32),
        ],
        compiler_params=_params(("arbitrary", "arbitrary", "arbitrary")),
        name="latent_attention",
    )(sink, mix, mix, mix, k_ctx, v_ctx, mix)


def _rope_tables(seq):
    rows = seq // GRID_W
    row = jnp.repeat(jnp.arange(rows, dtype=F32), GRID_W)
    colp = jnp.tile(jnp.arange(GRID_W, dtype=F32), rows)
    n_freq = HEAD_DIM // 4
    inv = ROPE_BASE ** (-jnp.arange(n_freq, dtype=F32) / n_freq)
    ang_r = row[:, None] * inv[None]
    ang_c = colp[:, None] * inv[None]
    cos_t = jnp.concatenate([jnp.cos(ang_r)] * 2 + [jnp.cos(ang_c)] * 2, axis=-1)
    sin_t = jnp.concatenate([-jnp.sin(ang_r), jnp.sin(ang_r), -jnp.sin(ang_c), jnp.sin(ang_c)], axis=-1)
    return cos_t, sin_t


def _row_bcast(x, r):
    return jnp.broadcast_to(x[r:r + 1, :], x.shape)


SCAN_ROWS = SUBLANES * SUBLANES


def _segment_scan(a_ref, b_ref, t0, carry, reverse):
    idx = [pl.ds(t0 + k, SUBLANES, stride=SUBLANES) for k in range(SUBLANES)]
    a = [a_ref[i, :] for i in idx]
    b = [b_ref[i, :] for i in idx]
    order = list(range(SUBLANES - 1, -1, -1)) if reverse else list(range(SUBLANES))
    h = [None] * SUBLANES
    p = [None] * SUBLANES
    prev = None
    for k in order:
        h[k] = b[k] if prev is None else a[k] * h[prev] + b[k]
        p[k] = a[k] if prev is None else a[k] * p[prev]
        prev = k
    pp, hh = p[prev], h[prev]
    sub = lax.broadcasted_iota(jnp.int32, (SUBLANES, LANES), 0)
    for d in (1, 2, 4):
        keep = (sub < SUBLANES - d) if reverse else (sub >= d)
        shift = SUBLANES - d if reverse else d
        p_sh = jnp.where(keep, pltpu.roll(pp, shift, 0), 1.0)
        h_sh = jnp.where(keep, pltpu.roll(hh, shift, 0), 0.0)
        hh = pp * h_sh + hh
        pp = pp * p_sh
    end = pp * carry + hh
    if reverse:
        c_in = jnp.where(sub < SUBLANES - 1, pltpu.roll(end, SUBLANES - 1, 0), carry)
    else:
        c_in = jnp.where(sub >= 1, pltpu.roll(end, 1, 0), carry)
    for k in range(SUBLANES):
        b_ref[idx[k], :] = h[k] + p[k] * c_in
    return _row_bcast(end, 0 if reverse else SUBLANES - 1)


def _lru_kernel(u_ref, z_ref, cw_ref, cb_ref, gw_ref, gb_ref, lam_ref, h0_ref, y_ref, hfin_ref,
                up_ref, af_ref, bf_ref, ab_ref, bb_ref, *, seq, chunk):
    pad = SUBLANES
    up_ref[0:pad, :] = jnp.zeros((pad, LANES), F32)
    up_ref[pad:pad + seq, :] = u_ref[...]
    up_ref[pad + seq:2 * pad + seq, :] = jnp.zeros((pad, LANES), F32)
    nlam = -lam_ref[...]
    softplus = jnp.maximum(nlam, 0.0) + jnp.log1p(jnp.exp(-jnp.abs(nlam)))
    half_rate = (-0.5 * LRU_C) * softplus

    def gates(ci, carry):
        rows = pl.ds(pl.multiple_of(ci * chunk, chunk), chunk)
        xc = cb_ref[...]
        for j in range(CONV_W):
            tap = up_ref[pl.ds(ci * chunk + (pad - CONV_LEFT + j), chunk), :]
            xc = xc + tap * cw_ref[j:j + 1, :]
        g = jnp.dot(xc.astype(BF16), gw_ref[...], preferred_element_type=F32) + gb_ref[...]
        xc_half = 0.5 * xc
        for d, (a_ref, b_ref) in enumerate(((af_ref, bf_ref), (ab_ref, bb_ref))):
            t_r = jnp.tanh(g[:, (2 * d) * LANES:(2 * d + 1) * LANES])
            t_i = jnp.tanh(g[:, (2 * d + 1) * LANES:(2 * d + 2) * LANES])
            log_a = (t_r + 1.0) * half_rate[d:d + 1, :]
            a = jnp.exp(log_a)
            one_minus_a2 = -jnp.tanh(log_a) * (a * a + 1.0)
            a_ref[rows, :] = a
            b_ref[rows, :] = jnp.sqrt(jnp.maximum(one_minus_a2, 0.0)) * ((t_i + 1.0) * xc_half)
        return carry

    lax.fori_loop(0, seq // chunk, gates, 0, unroll=min(4, seq // chunk))

    n_scan = seq // SCAN_ROWS

    def scan(ci, carry):
        c_f, c_b = carry
        c_f = _segment_scan(af_ref, bf_ref, ci * SCAN_ROWS, c_f, False)
        c_b = _segment_scan(ab_ref, bb_ref, (n_scan - 1 - ci) * SCAN_ROWS, c_b, True)
        return c_f, c_b

    h0 = h0_ref[...]
    c_f, c_b = lax.fori_loop(0, n_scan, scan, (jnp.broadcast_to(h0[0:1, :], (SUBLANES, LANES)),
                                               jnp.broadcast_to(h0[1:2, :], (SUBLANES, LANES))),
                             unroll=2)
    hfin_ref[0:1, :] = c_f[0:1, :]
    hfin_ref[1:2, :] = c_b[0:1, :]

    def gate_out(ci, carry):
        rows = pl.ds(pl.multiple_of(ci * chunk, chunk), chunk)
        y_ref[rows, :] = ((bf_ref[rows, :] + bb_ref[rows, :]) * z_ref[rows, :]).astype(BF16)
        return carry

    lax.fori_loop(0, seq // chunk, gate_out, 0, unroll=2)


def _rglru(mix, conv_w, conv_b, gate_w_b, gate_b, lam, h0, col, chunk):
    bsz, seq, _ = mix.shape
    assert seq % SCAN_ROWS == 0 and seq % chunk == 0
    lw = LRU_BLOCKS * LANES
    ucol = col["u_lru"] // LANES
    zcol = col["z_lru"] // LANES
    return pl.pallas_call(
        functools.partial(_lru_kernel, seq=seq, chunk=chunk),
        grid=(bsz, LRU_BLOCKS),
        in_specs=[
            pl.BlockSpec((None, seq, LANES), lambda b, n: (b, 0, ucol + n)),
            pl.BlockSpec((None, seq, LANES), lambda b, n: (b, 0, zcol + n)),
            pl.BlockSpec((CONV_W, LANES), lambda b, n: (0, n)),
            pl.BlockSpec((1, LANES), lambda b, n: (0, n)),
            pl.BlockSpec((None, LANES, 4 * LANES), lambda b, n: (n, 0, 0)),
            pl.BlockSpec((None, 1, 4 * LANES), lambda b, n: (n, 0, 0)),
            pl.BlockSpec((2, LANES), lambda b, n: (0, n)),
            pl.BlockSpec((None, 2, LANES), lambda b, n: (b, 0, n)),
        ],
        out_specs=[
            pl.BlockSpec((None, seq, LANES), lambda b, n: (b, 0, n)),
            pl.BlockSpec((None, 2, LANES), lambda b, n: (b, 0, n)),
        ],
        out_shape=[
            jax.ShapeDtypeStruct((bsz, seq, lw), BF16),
            jax.ShapeDtypeStruct((bsz, 2, lw), F32),
        ],
        scratch_shapes=[pltpu.VMEM((seq + 2 * SUBLANES, LANES), F32)]
        + [pltpu.VMEM((seq, LANES), F32)] * 4,
        compiler_params=_params(("arbitrary", "arbitrary")),
        name="rglru",
    )(mix, mix, conv_w, conv_b, gate_w_b, gate_b, lam, h0)


def _out_kernel(x_ref, h_ref, mod_ref, gpost_ref, yp_ref, ya_ref, yl_ref,
                wm0_ref, wm1_ref, wm2_ref, wp_ref, wa_ref, wl_ref, wo_ref, o_ref, mg_ref, acc_ref,
                *, n_merge, n_out, tn, tn_out):
    j = pl.program_id(1)

    @pl.when(j < n_merge)
    def _():
        h = h_ref[...]
        merged = None
        for wm_ref, y_ref, wb_ref in ((wm0_ref, yp_ref, wp_ref), (wm1_ref, ya_ref, wa_ref),
                                      (wm2_ref, yl_ref, wl_ref)):
            gate = _sigmoid(jnp.dot(h, wm_ref[...], preferred_element_type=F32))
            term = gate * jnp.dot(y_ref[...], wb_ref[...], preferred_element_type=F32)
            merged = term if merged is None else merged + term
        merged = merged.astype(BF16)
        for c in range(n_merge):
            @pl.when(j == c)
            def _():
                mg_ref[:, c * tn:(c + 1) * tn] = merged

    @pl.when(j == n_merge)
    def _():
        for c in range(n_out):
            cols = slice(c * tn_out, (c + 1) * tn_out)
            acc_ref[:, cols] = jnp.dot(mg_ref[...], wo_ref[:, cols], preferred_element_type=F32)
        gain = gpost_ref[...] * mod_ref[2]

        def body(c, carry):
            rows = pl.ds(pl.multiple_of(c * NORM_ROWS, NORM_ROWS), NORM_ROWS)
            out = acc_ref[rows, :]
            inv = lax.rsqrt(jnp.mean(out * out, axis=-1, keepdims=True) + EPS)
            o_ref[rows, :] = x_ref[rows, :] + (out * inv) * gain
            return carry

        lax.fori_loop(0, o_ref.shape[0] // NORM_ROWS, body, 0, unroll=NORM_UNROLL)


def _out_stage(x2, h2, mod, g_post, y_pool, y_attn, y_lru, w_in_b, w_pool_o, w_attn_o, w_lru_o,
               w_out, layer, n_mix, seq, mod_row0, tm, tn, tn_out):
    rows, d = x2.shape
    n_merge = d // tn
    n_out = d // tn_out
    if mod_row0 is None:
        mod_map = lambda i, j: ((i * tm) // seq, 0, 0, 0)
    else:
        mod_map = lambda i, j: (mod_row0, 0, 0, 0)
    mstep = lambda j: jnp.minimum(j, n_merge - 1)
    mcol = lambda br: (lambda i, j: (layer, 0, (n_mix + br * d) // tn + mstep(j)))
    bcol = lambda i, j: (layer, 0, mstep(j))
    return pl.pallas_call(
        functools.partial(_out_kernel, n_merge=n_merge, n_out=n_out, tn=tn, tn_out=tn_out),
        grid=(rows // tm, n_merge + 1),
        in_specs=[
            pl.BlockSpec((tm, d), lambda i, j: (i, 0)),
            pl.BlockSpec((tm, d), lambda i, j: (i, 0)),
            pl.BlockSpec((None, 3, 1, d), mod_map),
            pl.BlockSpec((1, d), lambda i, j: (0, 0)),
            pl.BlockSpec((tm, y_pool.shape[1]), lambda i, j: (i, 0)),
            pl.BlockSpec((tm, y_attn.shape[1]), lambda i, j: (i, 0)),
            pl.BlockSpec((tm, y_lru.shape[1]), lambda i, j: (i, 0)),
            pl.BlockSpec((None, d, tn), mcol(0)),
            pl.BlockSpec((None, d, tn), mcol(1)),
            pl.BlockSpec((None, d, tn), mcol(2)),
            pl.BlockSpec((None, w_pool_o.shape[1], tn), bcol),
            pl.BlockSpec((None, w_attn_o.shape[1], tn), bcol),
            pl.BlockSpec((None, w_lru_o.shape[1], tn), bcol),
            pl.BlockSpec((None, d, d), lambda i, j: (layer, 0, 0), pipeline_mode=pl.Buffered(1)),
        ],
        out_specs=pl.BlockSpec((tm, d), lambda i, j: (i, 0)),
        out_shape=jax.ShapeDtypeStruct((rows, d), F32),
        scratch_shapes=[pltpu.VMEM((tm, d), BF16), pltpu.VMEM((tm, d), F32)],
        compiler_params=_params(("arbitrary", "arbitrary")),
        name="out_stage",
    )(x2, h2, mod, g_post, y_pool, y_attn, y_lru, w_in_b, w_in_b, w_in_b,
      w_pool_o, w_attn_o, w_lru_o, w_out)


def _tile(n, want):
    t = min(n, want)
    assert n % t == 0, (n, t)
    return t


def _trunk_layer(x, mod, mod_row0, P, col, kinds, n_mix, ctx):
    bsz, seq, d = x.shape
    x2 = x.reshape(bsz * seq, d)
    rows = bsz * seq
    tm_in = _tile(rows if mod_row0 is not None else seq, 512)
    tables = None if ctx is None else ctx[3:]
    mix, h2, *new_kv = _in_projection(x2, mod, P["g_pre"], P["w_in"], P["layer"], n_mix, seq,
                                      mod_row0, tm_in, 768, kinds, tables)
    mix = mix.reshape(bsz, seq, n_mix)
    assert col["u_pool"] == 0 and col["z_pool"] == P["pool_scale"].shape[1]
    y_pool = _pool_mixer(mix, P["w_pool_map"], P["pool_scale"], _tile(seq, 1024))
    if ctx is None:
        y_attn = _context_attention(mix, P["attn_sink"], col)
        h0 = jnp.zeros((bsz, 2, LRU_BLOCKS * LANES), F32)
    else:
        k_ctx, v_ctx, h0 = ctx[:3]
        y_attn = _latent_attention(mix, P["attn_sink"], k_ctx, v_ctx, col)
    y_lru, h_fin = _rglru(mix, P["lru_conv_w"], P["lru_conv_b"], P["lru_gate_w"], P["lru_gate_b"],
                          P["lru_lambda"], h0, col, _tile(seq, 128))
    tm_out = _tile(rows if mod_row0 is not None else seq, 512)
    x_new = _out_stage(x2, h2, mod, P["g_post"], y_pool.reshape(rows, -1),
                       y_attn.reshape(rows, -1), y_lru.reshape(rows, -1), P["w_in"], P["w_pool_o"],
                       P["w_attn_o"], P["w_lru_o"], P["w_out"], P["layer"], n_mix, seq, mod_row0,
                       tm_out, 512, 512)
    new_kv = [a.reshape(bsz, seq, N_KV, HEAD_DIM) for a in new_kv]
    return x_new.reshape(bsz, seq, d), new_kv, h_fin


def kernel(x_prompt, x_sample, cache_k, cache_v, state_lru, c, c_ctx, g_pre, g_post, w_ada, b_ada,
           w_in, w_pool_map, pool_scale, attn_sink, lru_conv_w, lru_conv_b, lru_gate_w, lru_gate_b,
           lru_lambda, w_pool_o, w_attn_o, w_lru_o, w_out):
    depth = w_in.shape[0]
    d = x_prompt.shape[-1]
    dec_b, dec_seq, _ = x_sample.shape
    pool_w = len(POOL_WINDOWS) * LANES
    attn_w = N_HEADS * HEAD_DIM
    kv_w = N_KV * HEAD_DIM
    lru_w = LRU_BLOCKS * LANES
    widths = (("u_pool", pool_w), ("z_pool", pool_w), ("q", attn_w), ("k", kv_w), ("v", kv_w),
              ("z_attn", attn_w), ("u_lru", lru_w), ("z_lru", lru_w))
    col, n_mix, kinds = {}, 0, ()
    for name, w in widths:
        col[name] = n_mix
        n_mix += w
        kind = name[0] if name[0] in "zqkv" else ""
        kinds += tuple((kind, s) for s in range(w // LANES))
    assert w_in.shape[-1] == n_mix + N_BRANCH * d
    assert dec_b < MOD_ROWS

    cvec = jnp.zeros((MOD_ROWS, d), F32).at[:dec_b].set(c).at[dec_b].set(c_ctx)
    mod = _modulation(cvec, w_ada, b_ada)
    mod = mod.reshape(depth, MOD_ROWS, 3, 1, d)
    cos_t, sin_t = _rope_tables(dec_seq)
    past = cache_k.shape[2]
    cache_k2 = cache_k.reshape(dec_b, depth, past, kv_w)
    cache_v2 = cache_v.reshape(dec_b, depth, past, kv_w)

    w_in_b, w_out_b = w_in.astype(BF16), w_out.astype(BF16)
    w_pool_o_b, w_attn_o_b, w_lru_o_b = (w.astype(BF16) for w in (w_pool_o, w_attn_o, w_lru_o))

    xp, xs = x_prompt, x_sample
    new_k, new_v, new_h = [], [], []
    for l in range(depth):
        P = {
            "layer": l,
            "g_pre": g_pre[l][None], "g_post": g_post[l][None],
            "w_in": w_in_b,
            "w_pool_map": w_pool_map[l].astype(BF16), "pool_scale": pool_scale[l][None],
            "attn_sink": attn_sink[l],
            "lru_conv_w": lru_conv_w[l], "lru_conv_b": lru_conv_b[l][None],
            "lru_gate_w": (0.5 * lru_gate_w[l]).transpose(2, 3, 0, 1, 4).reshape(
                LRU_BLOCKS, LANES, 4 * LANES).astype(BF16),
            "lru_gate_b": (0.5 * lru_gate_b[l]).reshape(2, 2, LRU_BLOCKS, LANES).transpose(
                2, 0, 1, 3).reshape(LRU_BLOCKS, 1, 4 * LANES),
            "lru_lambda": lru_lambda[l],
            "w_pool_o": w_pool_o_b, "w_attn_o": w_attn_o_b, "w_lru_o": w_lru_o_b, "w_out": w_out_b,
        }
        xp, (k_l, v_l), h_fin = _trunk_layer(xp, mod[l], dec_b, P, col, kinds, n_mix, None)
        new_k.append(k_l)
        new_v.append(v_l)
        new_h.append(h_fin)
        ctx = (cache_k2[:, l], cache_v2[:, l], state_lru[:, l], cos_t, sin_t)
        xs, _, _ = _trunk_layer(xs, mod[l], None, P, col, kinds, n_mix, ctx)
    return (xp, xs, jnp.stack(new_k, axis=1), jnp.stack(new_v, axis=1), jnp.stack(new_h, axis=1))
```
